```python
import jax, jax.numpy as jnp
from jax import lax
import numpy as np

D_MODEL = 2048
BATCH = 2
SEQ = 4096
DEPTH = 1

ATTN_WIDTH = D_MODEL // 2
HEAD_DIM = 64
N_Q_HEADS = ATTN_WIDTH // HEAD_DIM
N_KV_HEADS = 2
GROUP = N_Q_HEADS // N_KV_HEADS
KV_WIDTH = N_KV_HEADS * HEAD_DIM
WINDOW = 128
BLOCK = 128
RNN_WIDTH = D_MODEL - ATTN_WIDTH
RNN_HEAD_DIM = 128
N_RNN_HEADS = RNN_WIDTH // RNN_HEAD_DIM
CHUNK = 64
MIX_WIDTH = ATTN_WIDTH + RNN_WIDTH
D_FF = 4 * D_MODEL
IN_WIDTH = ATTN_WIDTH + 2 * KV_WIDTH + 4 * RNN_WIDTH
SPLITS = tuple(int(s) for s in np.cumsum([ATTN_WIDTH, KV_WIDTH, KV_WIDTH,
                                           RNN_WIDTH, RNN_WIDTH, RNN_WIDTH]))
EPS = 1e-6

kernel_name = "hymba_swa_sink_hgrn2_sqrelu_sandwich"


def rmsnorm(x, gain):
    xf = x.astype(jnp.float32)
    y = xf * lax.rsqrt(jnp.mean(xf * xf, axis=-1, keepdims=True) + EPS)
    return (y * gain.astype(jnp.float32)).astype(x.dtype)


def alibi_slopes(n_heads):
    return jnp.exp2(-8.0 * jnp.arange(1, n_heads + 1, dtype=jnp.float32) / n_heads)


def sliding_window_attention(q, k, v, sinks):
    B, S, _ = q.shape
    nb = S // BLOCK
    qb = q.reshape(B, nb, BLOCK, N_KV_HEADS, GROUP, HEAD_DIM)
    kb = k.reshape(B, nb, BLOCK, N_KV_HEADS, HEAD_DIM)
    vb = v.reshape(B, nb, BLOCK, N_KV_HEADS, HEAD_DIM)
    pad = ((0, 0), (1, 0), (0, 0), (0, 0), (0, 0))
    kcat = jnp.concatenate([jnp.pad(kb, pad)[:, :-1], kb], axis=2)
    vcat = jnp.concatenate([jnp.pad(vb, pad)[:, :-1], vb], axis=2)
    scores = jnp.einsum('bnqhgd,bnkhd->bnhgqk', qb, kcat,
                        preferred_element_type=jnp.float32) * (HEAD_DIM ** -0.5)
    q_pos = jnp.arange(BLOCK) + BLOCK
    k_pos = jnp.arange(2 * BLOCK)
    dist = (q_pos[:, None] - k_pos[None, :]).astype(jnp.float32)
    band = (dist >= 0) & (dist < WINDOW)
    abs_k = jnp.arange(nb)[:, None] * BLOCK - BLOCK + k_pos[None, :]
    valid = band[None] & (abs_k >= 0)[:, None, :]
    slopes = alibi_slopes(N_Q_HEADS).reshape(N_KV_HEADS, GROUP, 1, 1)
    scores = scores - slopes * dist
    scores = jnp.where(valid[None, :, None, None], scores, -jnp.inf)
    sink = sinks.astype(jnp.float32).reshape(N_KV_HEADS, GROUP, 1, 1)
    m = jnp.maximum(jnp.max(scores, axis=-1, keepdims=True), sink)
    p = jnp.exp(scores - m)
    probs = p / (jnp.sum(p, axis=-1, keepdims=True) + jnp.exp(sink - m))
    out = jnp.einsum('bnhgqk,bnkhd->bnqhgd', probs.astype(v.dtype), vcat)
    return out.reshape(B, S, ATTN_WIDTH)


def hgrn2_chunkwise(q, f_logit, i, g, lb, norm_gain):
    B, S, _ = q.shape
    nc = S // CHUNK
    f32 = jnp.float32
    f = lb + (1.0 - lb) * jax.nn.sigmoid(f_logit.astype(f32))
    log_f = jnp.log(f)
    key = 1.0 - f
    qf = jax.nn.silu(q.astype(f32))
    vf = i.astype(f32)

    def to_chunks(t):
        return t.reshape(B, nc, CHUNK, N_RNN_HEADS, RNN_HEAD_DIM).transpose(1, 0, 3, 2, 4)

    causal = jnp.tril(jnp.ones((CHUNK, CHUNK), dtype=bool))

    def step(state, inp):
        qc, kc, vc, lfc = inp
        b = jnp.cumsum(lfc, axis=-2)
        o_inter = jnp.einsum('bhtk,bhkv->bhtv', qc * jnp.exp(b), state)
        diff = b[:, :, :, None, :] - b[:, :, None, :, :]
        decay = jnp.exp(jnp.where(causal[:, :, None], diff, -jnp.inf))
        att = jnp.einsum('bhtk,bhsk,bhtsk->bhts', qc, kc, decay)
        o_intra = jnp.einsum('bhts,bhsv->bhtv', att, vc)
        b_last = b[:, :, -1:, :]
        new_state = (jnp.exp(b_last[:, :, 0, :])[..., None] * state
                     + jnp.einsum('bhsk,bhsv->bhkv', kc * jnp.exp(b_last - b), vc))
        return new_state, o_inter + o_intra

    s0 = jnp.zeros((B, N_RNN_HEADS, RNN_HEAD_DIM, RNN_HEAD_DIM), f32)
    _, o = lax.scan(step, s0, (to_chunks(qf), to_chunks(key), to_chunks(vf), to_chunks(log_f)))
    o = o.transpose(1, 0, 3, 2, 4).reshape(B, S, N_RNN_HEADS, RNN_HEAD_DIM)
    o = o * lax.rsqrt(jnp.mean(o * o, axis=-1, keepdims=True) + EPS) * norm_gain.astype(f32)
    gate = jax.nn.silu(g.astype(f32)).reshape(B, S, N_RNN_HEADS, RNN_HEAD_DIM)
    return (o * gate).reshape(B, S, RNN_WIDTH).astype(q.dtype)


def setup_inputs(seed: int = 0) -> dict:
    key = jax.random.key(seed)
    ks = jax.random.split(key, 14)
    f32 = jnp.float32

    def gain(k, shape):
        return 1.0 + 0.05 * jax.random.normal(k, shape, f32)

    return {
        "x": jax.random.normal(ks[0], (BATCH, SEQ, D_MODEL), f32),
        "w_in": jax.random.normal(ks[1], (DEPTH, D_MODEL, IN_WIDTH), f32) * D_MODEL ** -0.5,
        "attn_sinks": 0.5 * jax.random.normal(ks[2], (DEPTH, N_Q_HEADS), f32),
        "attn_out_gain": gain(ks[3], (DEPTH, ATTN_WIDTH)),
        "rnn_lb_logits": 0.1 * jax.random.normal(ks[4], (DEPTH + 1, RNN_WIDTH), f32),
        "rnn_norm_gain": gain(ks[5], (DEPTH, RNN_HEAD_DIM)),
        "w_out": jax.random.normal(ks[6], (DEPTH, MIX_WIDTH, D_MODEL), f32) * MIX_WIDTH ** -0.5,
        "mix_pre_gain": gain(ks[7], (DEPTH, D_MODEL)),
        "mix_post_gain": gain(ks[8], (DEPTH, D_MODEL)),
        "mlp_pre_gain": gain(ks[9], (DEPTH, D_MODEL)),
        "mlp_post_gain": gain(ks[10], (DEPTH, D_MODEL)),
        "w_up": jax.random.normal(ks[11], (DEPTH, D_MODEL, D_FF), f32) * D_MODEL ** -0.5,
        "w_down": jax.random.normal(ks[12], (DEPTH, D_FF, D_MODEL), f32) * D_FF ** -0.5,
    }


def reference(x, w_in, attn_sinks, attn_out_gain, rnn_lb_logits, rnn_norm_gain, w_out,
              mix_pre_gain, mix_post_gain, mlp_pre_gain, mlp_post_gain, w_up, w_down):
    lb_all = jnp.cumsum(jax.nn.softmax(rnn_lb_logits.astype(jnp.float32), axis=0), axis=0)
    for layer in range(DEPTH):
        h = rmsnorm(x, mix_pre_gain[layer])
        proj = jnp.einsum('bsd,de->bse', h, w_in[layer])
        q_a, k_a, v_a, q_r, f_r, i_r, g_r = jnp.split(proj, SPLITS, axis=-1)
        attn = sliding_window_attention(q_a, k_a, v_a, attn_sinks[layer])
        attn = rmsnorm(attn, attn_out_gain[layer])
        rnn = hgrn2_chunkwise(q_r, f_r, i_r, g_r, lb_all[layer], rnn_norm_gain[layer])
        mixed = jnp.einsum('bse,ed->bsd', jnp.concatenate([attn, rnn], axis=-1), w_out[layer])
        x = x + rmsnorm(mixed, mix_post_gain[layer])
        h = rmsnorm(x, mlp_pre_gain[layer])
        u = jax.nn.relu(jnp.einsum('bsd,df->bsf', h, w_up[layer]))
        y = jnp.einsum('bsf,fd->bsd', u * u, w_down[layer])
        x = x + rmsnorm(y, mlp_post_gain[layer])
    return x
```

```python
import functools

import numpy as np
import jax
import jax.numpy as jnp
from jax import lax
from jax.experimental import pallas as pl
from jax.experimental.pallas import tpu as pltpu

D_MODEL = 2048
ATTN_WIDTH = 1024
HEAD_DIM = 64
N_Q_HEADS = 16
N_KV_HEADS = 2
KV_WIDTH = 128
BLOCK = 128
RNN_WIDTH = 1024
RNN_HEAD_DIM = 128
N_RNN_HEADS = 8
CHUNK = 64
D_FF = 8192
EPS = 1e-6

LANES = 128
VMEM_LIMIT = 56 * 1024 * 1024

MAIN_WIDTH = ATTN_WIDTH + 3 * RNN_WIDTH + 2 * KV_WIDTH
N_LEVELS = 6
NEG_BIG = -1e30

_NT = (((1,), (1,)), ((), ()))
_TN = (((0,), (0,)), ((), ()))


def _dot(a, b):
    return jnp.dot(a, b, preferred_element_type=jnp.float32)


def _dot_nt(a, b):
    return lax.dot_general(a, b, _NT, preferred_element_type=jnp.float32)


def _dot_tn(a, b):
    return lax.dot_general(a, b, _TN, preferred_element_type=jnp.float32)


def _rms_scale(y, width):
    return lax.rsqrt(jnp.sum(y * y, axis=-1, keepdims=True) * (1.0 / width) + EPS)


def _inproj_kernel(x_ref, gain_ref, w_ref, main_ref, f_ref):
    x = x_ref[...]
    h = (x * _rms_scale(x, D_MODEL) * gain_ref[...]).astype(jnp.bfloat16)
    for j in range(0, MAIN_WIDTH, 1024):
        wdt = min(1024, MAIN_WIDTH - j)
        main_ref[:, j:j + wdt] = _dot(h, w_ref[:, j:j + wdt]).astype(jnp.bfloat16)
    f_ref[...] = _dot(h, w_ref[:, MAIN_WIDTH:])


def _inproj(x2, gain, w, tm):
    t = x2.shape[0]
    n_all = w.shape[1]
    return pl.pallas_call(
        _inproj_kernel,
        out_shape=(jax.ShapeDtypeStruct((t, MAIN_WIDTH), jnp.bfloat16),
                   jax.ShapeDtypeStruct((t, RNN_WIDTH), jnp.float32)),
        grid=(t // tm,),
        in_specs=[
            pl.BlockSpec((tm, D_MODEL), lambda i: (i, 0)),
            pl.BlockSpec((1, D_MODEL), lambda i: (0, 0)),
            pl.BlockSpec((D_MODEL, n_all), lambda i: (0, 0), pipeline_mode=pl.Buffered(1)),
        ],
        out_specs=(pl.BlockSpec((tm, MAIN_WIDTH), lambda i: (i, 0)),
                   pl.BlockSpec((tm, RNN_WIDTH), lambda i: (i, 0))),
        compiler_params=pltpu.CompilerParams(
            dimension_semantics=("arbitrary",), vmem_limit_bytes=VMEM_LIMIT),
        name="inproj",
    )(x2, gain, w)


def _attn_kernel(sinks_ref, q_ref, kvp_ref, kvc_ref, bias_ref, gain_ref, o_ref, acc_ref):
    q = q_ref[...] * jnp.bfloat16(HEAD_DIM ** -0.5)
    kv = jnp.concatenate([kvp_ref[...], kvc_ref[...]], axis=0)
    h = HEAD_DIM
    kv_sw = jnp.concatenate([kv[:, h:2 * h], kv[:, 0:h], kv[:, 3 * h:4 * h], kv[:, 2 * h:3 * h]],
                            axis=1)
    lane = lax.broadcasted_iota(jnp.int32, kv.shape, 1)
    lo = (lane % LANES) < h
    zero = jnp.zeros_like(kv)
    placed = {
        (0, 0): jnp.where(lo, kv, zero), (0, 1): jnp.where(lo, zero, kv_sw),
        (1, 0): jnp.where(lo, kv_sw, zero), (1, 1): jnp.where(lo, zero, kv),
    }
    ssq = jnp.zeros((BLOCK, 1), jnp.float32)
    for c in range(ATTN_WIDTH // LANES):
        qc = q[:, c * LANES:(c + 1) * LANES]
        kvh = c // (ATTN_WIDTH // LANES // N_KV_HEADS)
        oc = jnp.zeros((BLOCK, LANES), jnp.float32)
        for par in range(2):
            head = 2 * c + par
            kx = placed[(kvh, par)][:, :LANES]
            vx = placed[(kvh, par)][:, LANES:]
            s = _dot_nt(qc, kx) + bias_ref[0, head]
            sink = sinks_ref[head]
            m = jnp.maximum(jnp.max(s, axis=1, keepdims=True), sink)
            p = jnp.exp(s - m)
            denom = jnp.sum(p, axis=1, keepdims=True) + jnp.exp(sink - m)
            p = (p * (1.0 / denom)).astype(jnp.bfloat16)
            oc = oc + _dot(p, vx)
        acc_ref[:, c * LANES:(c + 1) * LANES] = oc
        ssq = ssq + jnp.sum(oc * oc, axis=1, keepdims=True)
    scale = lax.rsqrt(ssq * (1.0 / ATTN_WIDTH) + EPS)
    o_ref[...] = (acc_ref[...] * scale * gain_ref[...]).astype(jnp.bfloat16)


def _attention(main, sinks, bias, gain, batch, seq):
    nb = seq // BLOCK
    kv_col = (ATTN_WIDTH + 3 * RNN_WIDTH) // (2 * KV_WIDTH)
    return pl.pallas_call(
        _attn_kernel,
        out_shape=jax.ShapeDtypeStruct((batch * seq, ATTN_WIDTH), jnp.bfloat16),
        grid=(batch, nb),
        in_specs=[
            pl.BlockSpec(memory_space=pltpu.SMEM),
            pl.BlockSpec((BLOCK, ATTN_WIDTH), lambda b, n: (b * nb + n, 0)),
            pl.BlockSpec((BLOCK, 2 * KV_WIDTH),
                         lambda b, n: (b * nb + jnp.maximum(n - 1, 0), kv_col)),
            pl.BlockSpec((BLOCK, 2 * KV_WIDTH), lambda b, n: (b * nb + n, kv_col)),
            pl.BlockSpec((1, N_Q_HEADS, BLOCK, 2 * BLOCK),
                         lambda b, n: (jnp.minimum(n, 1), 0, 0, 0)),
            pl.BlockSpec((1, ATTN_WIDTH), lambda b, n: (0, 0)),
        ],
        out_specs=pl.BlockSpec((BLOCK, ATTN_WIDTH), lambda b, n: (b * nb + n, 0)),
        scratch_shapes=[pltpu.VMEM((BLOCK, ATTN_WIDTH), jnp.float32)],
        compiler_params=pltpu.CompilerParams(
            dimension_semantics=("arbitrary", "arbitrary"), vmem_limit_bytes=VMEM_LIMIT),
        name="swa_attention",
    )(sinks, main, main, main, bias, gain)


def _attention_bias():
    q_pos = np.arange(BLOCK) + BLOCK
    k_pos = np.arange(2 * BLOCK)
    dist = (q_pos[:, None] - k_pos[None, :]).astype(np.float32)
    band = (dist >= 0) & (dist < BLOCK)
    valid = np.stack([band & (k_pos >= BLOCK)[None, :], band])
    slopes = jnp.exp2(-8.0 * jnp.arange(1, N_Q_HEADS + 1, dtype=jnp.float32) / N_Q_HEADS)
    alibi = -slopes[None, :, None, None] * jnp.asarray(dist)[None, None]
    return jnp.where(jnp.asarray(valid)[:, None], alibi, NEG_BIG)


def _hgrn_tables():
    c = CHUNK
    rows = []
    masks = []
    t = np.arange(c)
    for lvl in range(N_LEVELS):
        h = c >> (lvl + 1)
        odd = (t // h) % 2 == 1
        m = np.zeros((c, c), np.float32)
        for r in range(c):
            if odd[r]:
                m[r, (r // h) * h:r + 1] = 1.0
            else:
                m[r, r + 1:(r // h + 1) * h] = 1.0
        rows.append(m)
        same = (t[:, None] // (2 * h)) == (t[None, :] // (2 * h))
        masks.append((same & odd[:, None] & (~odd)[None, :]).astype(np.float32))
    rows.append(np.tril(np.ones((c, c), np.float32)))
    rows.append(np.triu(np.ones((c, c), np.float32), 1))
    masks.append(np.eye(c, dtype=np.float32))
    mall = np.concatenate(rows, axis=0)
    return np.concatenate([mall, mall], axis=1), np.stack(masks)


def _hgrn_kernel(lbl_ref, q_ref, i_ref, g_ref, f_ref, mall_ref, mask_ref, gain_ref,
                 o_ref, st_ref, e_ref, *, n_chunks):
    @pl.when(pl.program_id(1) == 0)
    def _():
        st_ref[...] = jnp.zeros_like(st_ref)

    lbl = lbl_ref[...]
    ex = jnp.exp(lbl - jnp.max(lbl, axis=0, keepdims=True))
    lb = ex[0:1] / jnp.sum(ex, axis=0, keepdims=True)
    gain = gain_ref[...]
    d = RNN_HEAD_DIM

    def chunk(ci, carry):
        r0 = pl.multiple_of(ci * CHUNK, CHUNK)
        rows = pl.ds(r0, CHUNK)
        fl = f_ref[rows, :]
        f = lb + (1.0 - lb) * (1.0 / (1.0 + jnp.exp(-fl)))
        lf = jnp.log(f)
        key = 1.0 - f
        hi = lf.astype(jnp.bfloat16)
        lo = (lf - hi.astype(jnp.float32)).astype(jnp.bfloat16)
        e_ref[...] = jnp.exp(_dot(mall_ref[...], jnp.concatenate([hi, lo], axis=0)))
        for hd in range(N_RNN_HEADS):
            cols = slice(hd * d, (hd + 1) * d)
            qx = q_ref[rows, cols].astype(jnp.float32)
            qf = qx * (1.0 / (1.0 + jnp.exp(-qx)))
            kh = key[:, cols]
            vh = i_ref[rows, cols]
            att = mask_ref[N_LEVELS] * _dot_nt(qf.astype(jnp.bfloat16), kh.astype(jnp.bfloat16))
            for lvl in range(N_LEVELS):
                e = e_ref[lvl * CHUNK:(lvl + 1) * CHUNK, cols]
                att = att + mask_ref[lvl] * _dot_nt((qf * e).astype(jnp.bfloat16),
                                                    (kh * e).astype(jnp.bfloat16))
            eb = e_ref[N_LEVELS * CHUNK:(N_LEVELS + 1) * CHUNK, cols]
            er = e_ref[(N_LEVELS + 1) * CHUNK:(N_LEVELS + 2) * CHUNK, cols]
            st = st_ref[hd]
            o = _dot(att.astype(jnp.bfloat16), vh) + _dot_nt((qf * eb).astype(jnp.bfloat16),
                                                             st.astype(jnp.bfloat16))
            st_ref[hd] = st * eb[CHUNK - 1:CHUNK, :] + _dot_tn(vh, (kh * er).astype(jnp.bfloat16))
            gx = g_ref[rows, cols].astype(jnp.float32)
            gate = gx * (1.0 / (1.0 + jnp.exp(-gx)))
            o_ref[rows, cols] = (o * _rms_scale(o, d) * gain * gate).astype(jnp.bfloat16)
        return carry

    lax.fori_loop(0, n_chunks, chunk, 0)


def _hgrn(main, f_r, lb_logits, gain, batch, seq, ts):
    mall, masks = _hgrn_tables()
    mall = jnp.asarray(mall, jnp.bfloat16)
    masks = jnp.asarray(masks)
    ns = seq // ts
    blk = lambda col: pl.BlockSpec((ts, RNN_WIDTH), lambda b, s: (b * ns + s, col))
    const = lambda shape: pl.BlockSpec(shape, lambda b, s: (0,) * len(shape))
    return pl.pallas_call(
        functools.partial(_hgrn_kernel, n_chunks=ts // CHUNK),
        out_shape=jax.ShapeDtypeStruct((batch * seq, RNN_WIDTH), jnp.bfloat16),
        grid=(batch, ns),
        in_specs=[
            const(lb_logits.shape),
            blk(1), blk(2), blk(3),
            pl.BlockSpec((ts, RNN_WIDTH), lambda b, s: (b * ns + s, 0)),
            const(mall.shape), const(masks.shape), const(gain.shape),
        ],
        out_specs=pl.BlockSpec((ts, RNN_WIDTH), lambda b, s: (b * ns + s, 0)),
        scratch_shapes=[
            pltpu.VMEM((N_RNN_HEADS, RNN_HEAD_DIM, RNN_HEAD_DIM), jnp.float32),
            pltpu.VMEM(((N_LEVELS + 2) * CHUNK, RNN_WIDTH), jnp.float32),
        ],
        compiler_params=pltpu.CompilerParams(
            dimension_semantics=("arbitrary", "arbitrary"), vmem_limit_bytes=VMEM_LIMIT),
        name="hgrn2",
    )(lb_logits, main, main, main, f_r, mall, masks, gain)


def _outproj_kernel(a_ref, r_ref, x_ref, w_ref, post_ref, pre_ref, x1_ref, h_ref):
    mixed = _dot(a_ref[...], w_ref[:ATTN_WIDTH, :]) + _dot(r_ref[...], w_ref[ATTN_WIDTH:, :])
    x1 = x_ref[...] + mixed * _rms_scale(mixed, D_MODEL) * post_ref[...]
    x1_ref[...] = x1
    h_ref[...] = (x1 * _rms_scale(x1, D_MODEL) * pre_ref[...]).astype(jnp.bfloat16)


def _outproj(attn, rnn, x2, w_out, post_gain, pre_gain, tm):
    t = x2.shape[0]
    row = lambda w: pl.BlockSpec((tm, w), lambda i: (i, 0))
    const = lambda shape, **kw: pl.BlockSpec(shape, lambda i: (0, 0), **kw)
    return pl.pallas_call(
        _outproj_kernel,
        out_shape=(jax.ShapeDtypeStruct((t, D_MODEL), jnp.float32),
                   jax.ShapeDtypeStruct((t, D_MODEL), jnp.bfloat16)),
        grid=(t // tm,),
        in_specs=[row(ATTN_WIDTH), row(RNN_WIDTH), row(D_MODEL),
                  const(w_out.shape, pipeline_mode=pl.Buffered(1)),
                  const((1, D_MODEL)), const((1, D_MODEL))],
        out_specs=(row(D_MODEL), row(D_MODEL)),
        compiler_params=pltpu.CompilerParams(
            dimension_semantics=("arbitrary",), vmem_limit_bytes=VMEM_LIMIT),
        name="outproj",
    )(attn, rnn, x2, w_out, post_gain, pre_gain)


def _mlp_kernel(h_ref, wu_ref, wd_ref, x1_ref, gain_ref, o_ref, acc_ref):
    f = pl.program_id(1)
    u = jnp.maximum(_dot(h_ref[...], wu_ref[...]), 0.0)
    part = _dot((u * u).astype(jnp.bfloat16), wd_ref[...])

    @pl.when(f == 0)
    def _():
        acc_ref[...] = part

    @pl.when(f > 0)
    def _():
        acc_ref[...] += part

    @pl.when(f == pl.num_programs(1) - 1)
    def _():
        y = acc_ref[...]
        o_ref[...] = x1_ref[...] + y * _rms_scale(y, D_MODEL) * gain_ref[...]


def _mlp(h2, w_up, w_down, x1, gain, tm, tf):
    t = h2.shape[0]
    return pl.pallas_call(
        _mlp_kernel,
        out_shape=jax.ShapeDtypeStruct((t, D_MODEL), jnp.float32),
        grid=(t // tm, D_FF // tf),
        in_specs=[
            pl.BlockSpec((tm, D_MODEL), lambda i, f: (i, 0)),
            pl.BlockSpec((D_MODEL, tf), lambda i, f: (0, f)),
            pl.BlockSpec((tf, D_MODEL), lambda i, f: (f, 0)),
            pl.BlockSpec((tm, D_MODEL), lambda i, f: (i, 0)),
            pl.BlockSpec((1, D_MODEL), lambda i, f: (0, 0)),
        ],
        out_specs=pl.BlockSpec((tm, D_MODEL), lambda i, f: (i, 0)),
        scratch_shapes=[pltpu.VMEM((tm, D_MODEL), jnp.float32)],
        compiler_params=pltpu.CompilerParams(
            dimension_semantics=("arbitrary", "arbitrary"), vmem_limit_bytes=VMEM_LIMIT),
        name="mlp",
    )(h2, w_up, w_down, x1, gain)


def kernel(x, w_in, attn_sinks, attn_out_gain, rnn_lb_logits, rnn_norm_gain, w_out,
           mix_pre_gain, mix_post_gain, mlp_pre_gain, mlp_post_gain, w_up, w_down):
    batch, seq, _ = x.shape
    layer = 0
    x2 = x.reshape(batch * seq, D_MODEL)
    a, kv, r = ATTN_WIDTH, 2 * KV_WIDTH, RNN_WIDTH
    w = w_in[layer]
    w_perm = jnp.concatenate(
        [w[:, :a], w[:, a + kv:a + kv + r], w[:, a + kv + 2 * r:a + kv + 3 * r],
         w[:, a + kv + 3 * r:], w[:, a:a + kv], w[:, a + kv + r:a + kv + 2 * r]],
        axis=1).astype(jnp.bfloat16)

    main, f_r = _inproj(x2, mix_pre_gain[layer][None], w_perm, tm=512)
    attn = _attention(main, attn_sinks[layer], _attention_bias(), attn_out_gain[layer][None],
                      batch, seq)
    rnn = _hgrn(main, f_r, rnn_lb_logits, rnn_norm_gain[layer][None], batch, seq, ts=512)
    x1, h2 = _outproj(attn, rnn, x2, w_out[layer].astype(jnp.bfloat16),
                      mix_post_gain[layer][None], mlp_pre_gain[layer][None], tm=512)
    out = _mlp(h2, w_up[layer].astype(jnp.bfloat16), w_down[layer].astype(jnp.bfloat16),
               x1, mlp_post_gain[layer][None], tm=512, tf=1024)
    return out.reshape(batch, seq, D_MODEL)
```

```python
import functools

import numpy as np
import jax
import jax.numpy as jnp
from jax import lax
from jax.experimental import pallas as pl
from jax.experimental.pallas import tpu as pltpu

D_MODEL = 2048
ATTN_WIDTH = 1024
HEAD_DIM = 64
N_Q_HEADS = 16
N_KV_HEADS = 2
KV_WIDTH = 128
BLOCK = 128
RNN_WIDTH = 1024
RNN_HEAD_DIM = 128
N_RNN_HEADS = 8
CHUNK = 64
D_FF = 8192
EPS = 1e-6

LANES = 128
CHUNKS_PER_KV = ATTN_WIDTH // LANES // N_KV_HEADS
VMEM_LIMIT = 56 * 1024 * 1024

MAIN_WIDTH = ATTN_WIDTH + 3 * RNN_WIDTH + 2 * KV_WIDTH
N_LEVELS = 6
NEG_BIG = -1e30

_NT = (((1,), (1,)), ((), ()))
_TN = (((0,), (0,)), ((), ()))


def _dot(a, b):
    return jnp.dot(a, b, preferred_element_type=jnp.float32)


def _dot_nt(a, b):
    return lax.dot_general(a, b, _NT, preferred_element_type=jnp.float32)


def _dot_tn(a, b):
    return lax.dot_general(a, b, _TN, preferred_element_type=jnp.float32)


def _rms_scale(y, width):
    return lax.rsqrt(jnp.sum(y * y, axis=-1, keepdims=True) * (1.0 / width) + EPS)


def _inproj_kernel(x_ref, gain_ref, w_ref, main_ref, f_ref):
    x = x_ref[...]
    h = (x * _rms_scale(x, D_MODEL) * gain_ref[...]).astype(jnp.bfloat16)
    for j in range(0, MAIN_WIDTH, 1024):
        wdt = min(1024, MAIN_WIDTH - j)
        main_ref[:, j:j + wdt] = _dot(h, w_ref[:, j:j + wdt]).astype(jnp.bfloat16)
    f_ref[...] = _dot(h, w_ref[:, MAIN_WIDTH:])


def _inproj(x2, gain, w, tm):
    t = x2.shape[0]
    n_all = w.shape[1]
    return pl.pallas_call(
        _inproj_kernel,
        out_shape=(jax.ShapeDtypeStruct((t, MAIN_WIDTH), jnp.bfloat16),
                   jax.ShapeDtypeStruct((t, RNN_WIDTH), jnp.float32)),
        grid=(t // tm,),
        in_specs=[
            pl.BlockSpec((tm, D_MODEL), lambda i: (i, 0)),
            pl.BlockSpec((1, D_MODEL), lambda i: (0, 0)),
            pl.BlockSpec((D_MODEL, n_all), lambda i: (0, 0), pipeline_mode=pl.Buffered(1)),
        ],
        out_specs=(pl.BlockSpec((tm, MAIN_WIDTH), lambda i: (i, 0)),
                   pl.BlockSpec((tm, RNN_WIDTH), lambda i: (i, 0))),
        compiler_params=pltpu.CompilerParams(
            dimension_semantics=("arbitrary",), vmem_limit_bytes=VMEM_LIMIT),
        name="inproj",
    )(x2, gain, w)


def _attn_kernel(q_ref, kvp_ref, kvc_ref, bias_ref, gain_ref, o_ref, acc_ref):
    q = q_ref[...] * jnp.bfloat16(HEAD_DIM ** -0.5)
    kv = jnp.concatenate([kvp_ref[...], kvc_ref[...]], axis=0)
    h = HEAD_DIM
    kv_sw = jnp.concatenate([kv[:, h:2 * h], kv[:, 0:h], kv[:, 3 * h:4 * h], kv[:, 2 * h:3 * h]],
                            axis=1)
    lane = lax.broadcasted_iota(jnp.int32, kv.shape, 1)
    row = lax.broadcasted_iota(jnp.int32, kv.shape, 0)
    lo = ((lane % LANES) < h) & (row > 0)
    hi = ((lane % LANES) >= h) & (row > 0)
    zero = jnp.zeros_like(kv)
    placed = {
        (0, 0): jnp.where(lo, kv, zero), (0, 1): jnp.where(hi, kv_sw, zero),
        (1, 0): jnp.where(lo, kv_sw, zero), (1, 1): jnp.where(hi, kv, zero),
    }
    ones = jnp.ones((2 * BLOCK, LANES), jnp.bfloat16)
    sq = jnp.zeros((BLOCK, LANES), jnp.float32)
    for kvh in range(N_KV_HEADS):
        chunks = [CHUNKS_PER_KV * kvh + j for j in range(CHUNKS_PER_KV)]
        q_stack = jnp.concatenate([q[:, c * LANES:(c + 1) * LANES] for c in chunks], axis=0)
        out = None
        for par in range(2):
            kx = placed[(kvh, par)][:, :LANES]
            vx = jnp.concatenate([placed[(kvh, par)][:, LANES:], ones], axis=1)
            s = _dot_nt(q_stack, kx) + bias_ref[0, 2 * kvh + par]
            s0, s1 = s[:, :BLOCK], s[:, BLOCK:]
            m = jnp.max(jnp.maximum(s0, s1), axis=1, keepdims=True)
            m = jnp.broadcast_to(m, s0.shape)
            p = jnp.concatenate([jnp.exp(s0 - m), jnp.exp(s1 - m)], axis=1).astype(jnp.bfloat16)
            r = _dot(p, vx)
            contrib = r[:, :LANES] * (1.0 / r[:, LANES:])
            out = contrib if out is None else out + contrib
        for j, c in enumerate(chunks):
            oc = out[j * BLOCK:(j + 1) * BLOCK]
            acc_ref[:, c * LANES:(c + 1) * LANES] = oc
            sq = sq + oc * oc
    scale = lax.rsqrt(jnp.sum(sq, axis=1, keepdims=True) * (1.0 / ATTN_WIDTH) + EPS)
    o_ref[...] = (acc_ref[...] * scale * gain_ref[...]).astype(jnp.bfloat16)


def _attention(main, bias, gain, batch, seq):
    nb = seq // BLOCK
    kv_col = (ATTN_WIDTH + 3 * RNN_WIDTH) // (2 * KV_WIDTH)
    return pl.pallas_call(
        _attn_kernel,
        out_shape=jax.ShapeDtypeStruct((batch * seq, ATTN_WIDTH), jnp.bfloat16),
        grid=(batch, nb),
        in_specs=[
            pl.BlockSpec((BLOCK, ATTN_WIDTH), lambda b, n: (b * nb + n, 0)),
            pl.BlockSpec((BLOCK, 2 * KV_WIDTH),
                         lambda b, n: (b * nb + jnp.maximum(n - 1, 0), kv_col)),
            pl.BlockSpec((BLOCK, 2 * KV_WIDTH), lambda b, n: (b * nb + n, kv_col)),
            pl.BlockSpec((1, 2 * N_KV_HEADS, CHUNKS_PER_KV * BLOCK, 2 * BLOCK),
                         lambda b, n: (jnp.minimum(n, 1), 0, 0, 0)),
            pl.BlockSpec((1, ATTN_WIDTH), lambda b, n: (0, 0)),
        ],
        out_specs=pl.BlockSpec((BLOCK, ATTN_WIDTH), lambda b, n: (b * nb + n, 0)),
        scratch_shapes=[pltpu.VMEM((BLOCK, ATTN_WIDTH), jnp.float32)],
        compiler_params=pltpu.CompilerParams(
            dimension_semantics=("arbitrary", "arbitrary"), vmem_limit_bytes=VMEM_LIMIT),
        name="swa_attention",
    )(main, main, main, bias, gain)


def _attention_bias(sinks):
    q_pos = np.arange(BLOCK) + BLOCK
    k_pos = np.arange(2 * BLOCK)
    dist = (q_pos[:, None] - k_pos[None, :]).astype(np.float32)
    band = (dist >= 0) & (dist < BLOCK)
    assert not band[:, 0].any()
    valid = np.stack([band & (k_pos >= BLOCK)[None, :], band])
    slopes = jnp.exp2(-8.0 * jnp.arange(1, N_Q_HEADS + 1, dtype=jnp.float32) / N_Q_HEADS)
    alibi = -slopes[None, :, None, None] * jnp.asarray(dist)[None, None]
    bias = jnp.where(jnp.asarray(valid)[:, None], alibi, NEG_BIG)
    sink_slot = jnp.asarray(k_pos == 0)[None, None, None, :]
    bias = jnp.where(sink_slot, sinks.astype(jnp.float32)[None, :, None, None], bias)
    bias = bias.reshape(2, N_KV_HEADS, CHUNKS_PER_KV, 2, BLOCK, 2 * BLOCK)
    return bias.transpose(0, 1, 3, 2, 4, 5).reshape(
        2, 2 * N_KV_HEADS, CHUNKS_PER_KV * BLOCK, 2 * BLOCK)


def _hgrn_tables():
    c = CHUNK
    rows = []
    masks = []
    t = np.arange(c)
    for lvl in range(N_LEVELS):
        h = c >> (lvl + 1)
        odd = (t // h) % 2 == 1
        m = np.zeros((c, c), np.float32)
        for r in range(c):
            if odd[r]:
                m[r, (r // h) * h:r + 1] = 1.0
            else:
                m[r, r + 1:(r // h + 1) * h] = 1.0
        rows.append(m)
        same = (t[:, None] // (2 * h)) == (t[None, :] // (2 * h))
        masks.append((same & odd[:, None] & (~odd)[None, :]).astype(np.float32))
    rows.append(np.tril(np.ones((c, c), np.float32)))
    rows.append(np.triu(np.ones((c, c), np.float32), 1))
    masks.append(np.eye(c, dtype=np.float32))
    mall = np.concatenate(rows, axis=0)
    return np.concatenate([mall, mall], axis=1), np.stack(masks)


def _hgrn_kernel(lbl_ref, q_ref, i_ref, g_ref, f_ref, mall_ref, mask_ref, gain_ref,
                 o_ref, st_ref, e_ref, *, n_chunks):
    @pl.when(pl.program_id(1) == 0)
    def _():
        st_ref[...] = jnp.zeros_like(st_ref)

    lbl = lbl_ref[...]
    ex = jnp.exp(lbl - jnp.max(lbl, axis=0, keepdims=True))
    lb = ex[0:1] / jnp.sum(ex, axis=0, keepdims=True)
    gain = gain_ref[...]
    d = RNN_HEAD_DIM

    def chunk(ci, carry):
        r0 = pl.multiple_of(ci * CHUNK, CHUNK)
        rows = pl.ds(r0, CHUNK)
        fl = f_ref[rows, :]
        f = lb + (1.0 - lb) * (1.0 / (1.0 + jnp.exp(-fl)))
        lf = jnp.log(f)
        key = 1.0 - f
        hi = lf.astype(jnp.bfloat16)
        lo = (lf - hi.astype(jnp.float32)).astype(jnp.bfloat16)
        e_ref[...] = jnp.exp(_dot(mall_ref[...], jnp.concatenate([hi, lo], axis=0)))
        for hd in range(N_RNN_HEADS):
            cols = slice(hd * d, (hd + 1) * d)
            qx = q_ref[rows, cols].astype(jnp.float32)
            qf = qx * (1.0 / (1.0 + jnp.exp(-qx)))
            kh = key[:, cols]
            vh = i_ref[rows, cols]
            att = mask_ref[N_LEVELS] * _dot_nt(qf.astype(jnp.bfloat16), kh.astype(jnp.bfloat16))
            for lvl in range(N_LEVELS):
                e = e_ref[lvl * CHUNK:(lvl + 1) * CHUNK, cols]
                att = att + mask_ref[lvl] * _dot_nt((qf * e).astype(jnp.bfloat16),
                                                    (kh * e).astype(jnp.bfloat16))
            eb = e_ref[N_LEVELS * CHUNK:(N_LEVELS + 1) * CHUNK, cols]
            er = e_ref[(N_LEVELS + 1) * CHUNK:(N_LEVELS + 2) * CHUNK, cols]
            st = st_ref[hd]
            o = _dot(att.astype(jnp.bfloat16), vh) + _dot_nt((qf * eb).astype(jnp.bfloat16),
                                                             st.astype(jnp.bfloat16))
            st_ref[hd] = st * eb[CHUNK - 1:CHUNK, :] + _dot_tn(vh, (kh * er).astype(jnp.bfloat16))
            gx = g_ref[rows, cols].astype(jnp.float32)
            gate = gx * (1.0 / (1.0 + jnp.exp(-gx)))
            o_ref[rows, cols] = (o * _rms_scale(o, d) * gain * gate).astype(jnp.bfloat16)
        return carry

    lax.fori_loop(0, n_chunks, chunk, 0)


def _hgrn(main, f_r, lb_logits, gain, batch, seq, ts):
    mall, masks = _hgrn_tables()
    mall = jnp.asarray(mall, jnp.bfloat16)
    masks = jnp.asarray(masks)
    ns = seq // ts
    blk = lambda col: pl.BlockSpec((ts, RNN_WIDTH), lambda b, s: (b * ns + s, col))
    const = lambda shape: pl.BlockSpec(shape, lambda b, s: (0,) * len(shape))
    return pl.pallas_call(
        functools.partial(_hgrn_kernel, n_chunks=ts // CHUNK),
        out_shape=jax.ShapeDtypeStruct((batch * seq, RNN_WIDTH), jnp.bfloat16),
        grid=(batch, ns),
        in_specs=[
            const(lb_logits.shape),
            blk(1), blk(2), blk(3),
            pl.BlockSpec((ts, RNN_WIDTH), lambda b, s: (b * ns + s, 0)),
            const(mall.shape), const(masks.shape), const(gain.shape),
        ],
        out_specs=pl.BlockSpec((ts, RNN_WIDTH), lambda b, s: (b * ns + s, 0)),
        scratch_shapes=[
            pltpu.VMEM((N_RNN_HEADS, RNN_HEAD_DIM, RNN_HEAD_DIM), jnp.float32),
            pltpu.VMEM(((N_LEVELS + 2) * CHUNK, RNN_WIDTH), jnp.float32),
        ],
        compiler_params=pltpu.CompilerParams(
            dimension_semantics=("arbitrary", "arbitrary"), vmem_limit_bytes=VMEM_LIMIT),
        name="hgrn2",
    )(lb_logits, main, main, main, f_r, mall, masks, gain)


def _outproj_kernel(a_ref, r_ref, x_ref, w_ref, post_ref, pre_ref, x1_ref, h_ref):
    mixed = _dot(a_ref[...], w_ref[:ATTN_WIDTH, :]) + _dot(r_ref[...], w_ref[ATTN_WIDTH:, :])
    x1 = x_ref[...] + mixed * _rms_scale(mixed, D_MODEL) * post_ref[...]
    x1_ref[...] = x1
    h_ref[...] = (x1 * _rms_scale(x1, D_MODEL) * pre_ref[...]).astype(jnp.bfloat16)


def _outproj(attn, rnn, x2, w_out, post_gain, pre_gain, tm):
    t = x2.shape[0]
    row = lambda w: pl.BlockSpec((tm, w), lambda i: (i, 0))
    const = lambda shape, **kw: pl.BlockSpec(shape, lambda i: (0, 0), **kw)
    return pl.pallas_call(
        _outproj_kernel,
        out_shape=(jax.ShapeDtypeStruct((t, D_MODEL), jnp.float32),
                   jax.ShapeDtypeStruct((t, D_MODEL), jnp.bfloat16)),
        grid=(t // tm,),
        in_specs=[row(ATTN_WIDTH), row(RNN_WIDTH), row(D_MODEL),
                  const(w_out.shape, pipeline_mode=pl.Buffered(1)),
                  const((1, D_MODEL)), const((1, D_MODEL))],
        out_specs=(row(D_MODEL), row(D_MODEL)),
        compiler_params=pltpu.CompilerParams(
            dimension_semantics=("arbitrary",), vmem_limit_bytes=VMEM_LIMIT),
        name="outproj",
    )(attn, rnn, x2, w_out, post_gain, pre_gain)


def _mlp_kernel(h_ref, wu_ref, wd_ref, x1_ref, gain_ref, o_ref, acc_ref):
    f = pl.program_id(1)
    u = jnp.maximum(_dot(h_ref[...], wu_ref[...]), 0.0)
    part = _dot((u * u).astype(jnp.bfloat16), wd_ref[...])

    @pl.when(f == 0)
    def _():
        acc_ref[...] = part

    @pl.when(f > 0)
    def _():
        acc_ref[...] += part

    @pl.when(f == pl.num_programs(1) - 1)
    def _():
        y = acc_ref[...]
        o_ref[...] = x1_ref[...] + y * _rms_scale(y, D_MODEL) * gain_ref[...]


def _mlp(h2, w_up, w_down, x1, gain, tm, tf):
    t = h2.shape[0]
    return pl.pallas_call(
        _mlp_kernel,
        out_shape=jax.ShapeDtypeStruct((t, D_MODEL), jnp.float32),
        grid=(t // tm, D_FF // tf),
        in_specs=[
            pl.BlockSpec((tm, D_MODEL), lambda i, f: (i, 0)),
            pl.BlockSpec((D_MODEL, tf), lambda i, f: (0, f)),
            pl.BlockSpec((tf, D_MODEL), lambda i, f: (f, 0)),
            pl.BlockSpec((tm, D_MODEL), lambda i, f: (i, 0)),
            pl.BlockSpec((1, D_MODEL), lambda i, f: (0, 0)),
        ],
        out_specs=pl.BlockSpec((tm, D_MODEL), lambda i, f: (i, 0)),
        scratch_shapes=[pltpu.VMEM((tm, D_MODEL), jnp.float32)],
        compiler_params=pltpu.CompilerParams(
            dimension_semantics=("arbitrary", "arbitrary"), vmem_limit_bytes=VMEM_LIMIT),
        name="mlp",
    )(h2, w_up, w_down, x1, gain)


def kernel(x, w_in, attn_sinks, attn_out_gain, rnn_lb_logits, rnn_norm_gain, w_out,
           mix_pre_gain, mix_post_gain, mlp_pre_gain, mlp_post_gain, w_up, w_down):
    batch, seq, _ = x.shape
    layer = 0
    x2 = x.reshape(batch * seq, D_MODEL)
    a, kv, r = ATTN_WIDTH, 2 * KV_WIDTH, RNN_WIDTH
    w = w_in[layer]
    w_perm = jnp.concatenate(
        [w[:, :a], w[:, a + kv:a + kv + r], w[:, a + kv + 2 * r:a + kv + 3 * r],
         w[:, a + kv + 3 * r:], w[:, a:a + kv], w[:, a + kv + r:a + kv + 2 * r]],
        axis=1).astype(jnp.bfloat16)

    main, f_r = _inproj(x2, mix_pre_gain[layer][None], w_perm, tm=512)
    attn = _attention(main, _attention_bias(attn_sinks[layer]), attn_out_gain[layer][None], batch, seq)
    rnn = _hgrn(main, f_r, rnn_lb_logits, rnn_norm_gain[layer][None], batch, seq, ts=512)
    x1, h2 = _outproj(attn, rnn, x2, w_out[layer].astype(jnp.bfloat16),
                      mix_post_gain[layer][None], mlp_pre_gain[layer][None], tm=512)
    out = _mlp(h2, w_up[layer].astype(jnp.bfloat16), w_down[layer].astype(jnp.bfloat16),
               x1, mlp_post_gain[layer][None], tm=512, tf=1024)
    return out.reshape(batch, seq, D_MODEL)
```

```python
import functools

import numpy as np
import jax
import jax.numpy as jnp
from jax import lax
from jax.experimental import pallas as pl
from jax.experimental.pallas import tpu as pltpu

D_MODEL = 2048
ATTN_WIDTH = 1024
HEAD_DIM = 64
N_Q_HEADS = 16
N_KV_HEADS = 2
KV_WIDTH = 128
BLOCK = 128
RNN_WIDTH = 1024
RNN_HEAD_DIM = 128
N_RNN_HEADS = 8
CHUNK = 64
D_FF = 8192
EPS = 1e-6

LANES = 128
CHUNKS_PER_KV = ATTN_WIDTH // LANES // N_KV_HEADS
VMEM_LIMIT = 56 * 1024 * 1024

MAIN_WIDTH = ATTN_WIDTH + 3 * RNN_WIDTH + 2 * KV_WIDTH
N_LEVELS = 6
NEG_BIG = -1e30

_NT = (((1,), (1,)), ((), ()))
_TN = (((0,), (0,)), ((), ()))


def _dot(a, b):
    return jnp.dot(a, b, preferred_element_type=jnp.float32)


def _dot_nt(a, b):
    return lax.dot_general(a, b, _NT, preferred_element_type=jnp.float32)


def _dot_tn(a, b):
    return lax.dot_general(a, b, _TN, preferred_element_type=jnp.float32)


def _rms_scale(y, width):
    return lax.rsqrt(jnp.sum(y * y, axis=-1, keepdims=True) * (1.0 / width) + EPS)


def _inproj_kernel(x_ref, gain_ref, w_ref, main_ref, f_ref):
    x = x_ref[...]
    h = (x * _rms_scale(x, D_MODEL) * gain_ref[...]).astype(jnp.bfloat16)
    for j in range(0, MAIN_WIDTH, 1024):
        wdt = min(1024, MAIN_WIDTH - j)
        main_ref[:, j:j + wdt] = _dot(h, w_ref[:, j:j + wdt]).astype(jnp.bfloat16)
    f_ref[...] = _dot(h, w_ref[:, MAIN_WIDTH:])


def _inproj(x2, gain, w, tm):
    t = x2.shape[0]
    n_all = w.shape[1]
    return pl.pallas_call(
        _inproj_kernel,
        out_shape=(jax.ShapeDtypeStruct((t, MAIN_WIDTH), jnp.bfloat16),
                   jax.ShapeDtypeStruct((t, RNN_WIDTH), jnp.float32)),
        grid=(t // tm,),
        in_specs=[
            pl.BlockSpec((tm, D_MODEL), lambda i: (i, 0)),
            pl.BlockSpec((1, D_MODEL), lambda i: (0, 0)),
            pl.BlockSpec((D_MODEL, n_all), lambda i: (0, 0), pipeline_mode=pl.Buffered(1)),
        ],
        out_specs=(pl.BlockSpec((tm, MAIN_WIDTH), lambda i: (i, 0)),
                   pl.BlockSpec((tm, RNN_WIDTH), lambda i: (i, 0))),
        compiler_params=pltpu.CompilerParams(
            dimension_semantics=("arbitrary",), vmem_limit_bytes=VMEM_LIMIT),
        name="inproj",
    )(x2, gain, w)


def _attn_kernel(q_ref, kvp_ref, kvc_ref, bias_ref, gain_ref, o_ref, acc_ref):
    q = q_ref[...] * jnp.bfloat16(HEAD_DIM ** -0.5)
    kv = jnp.concatenate([kvp_ref[...], kvc_ref[...]], axis=0)
    h = HEAD_DIM
    kv_sw = jnp.concatenate([kv[:, h:2 * h], kv[:, 0:h], kv[:, 3 * h:4 * h], kv[:, 2 * h:3 * h]],
                            axis=1)
    lane = lax.broadcasted_iota(jnp.int32, kv.shape, 1)
    row = lax.broadcasted_iota(jnp.int32, kv.shape, 0)
    lo = ((lane % LANES) < h) & (row > 0)
    hi = ((lane % LANES) >= h) & (row > 0)
    zero = jnp.zeros_like(kv)
    placed = {
        (0, 0): jnp.where(lo, kv, zero), (0, 1): jnp.where(hi, kv_sw, zero),
        (1, 0): jnp.where(lo, kv_sw, zero), (1, 1): jnp.where(hi, kv, zero),
    }
    ones = jnp.ones((2 * BLOCK, LANES), jnp.bfloat16)
    sq = jnp.zeros((BLOCK, LANES), jnp.float32)
    for kvh in range(N_KV_HEADS):
        chunks = [CHUNKS_PER_KV * kvh + j for j in range(CHUNKS_PER_KV)]
        q_stack = jnp.concatenate([q[:, c * LANES:(c + 1) * LANES] for c in chunks], axis=0)
        out = None
        for par in range(2):
            kx = placed[(kvh, par)][:, :LANES]
            vx = jnp.concatenate([placed[(kvh, par)][:, LANES:], ones], axis=1)
            s = _dot_nt(q_stack, kx) + bias_ref[0, 2 * kvh + par]
            s0, s1 = s[:, :BLOCK], s[:, BLOCK:]
            m = jnp.max(jnp.maximum(s0, s1), axis=1, keepdims=True)
            m = jnp.broadcast_to(m, s0.shape)
            p = jnp.concatenate([jnp.exp(s0 - m), jnp.exp(s1 - m)], axis=1).astype(jnp.bfloat16)
            r = _dot(p, vx)
            contrib = r[:, :LANES] * (1.0 / r[:, LANES:])
            out = contrib if out is None else out + contrib
        for j, c in enumerate(chunks):
            oc = out[j * BLOCK:(j + 1) * BLOCK]
            acc_ref[:, c * LANES:(c + 1) * LANES] = oc
            sq = sq + oc * oc
    scale = lax.rsqrt(jnp.sum(sq, axis=1, keepdims=True) * (1.0 / ATTN_WIDTH) + EPS)
    o_ref[...] = (acc_ref[...] * scale * gain_ref[...]).astype(jnp.bfloat16)


def _attention(main, bias, gain, batch, seq):
    nb = seq // BLOCK
    kv_col = (ATTN_WIDTH + 3 * RNN_WIDTH) // (2 * KV_WIDTH)
    return pl.pallas_call(
        _attn_kernel,
        out_shape=jax.ShapeDtypeStruct((batch * seq, ATTN_WIDTH), jnp.bfloat16),
        grid=(batch, nb),
        in_specs=[
            pl.BlockSpec((BLOCK, ATTN_WIDTH), lambda b, n: (b * nb + n, 0)),
            pl.BlockSpec((BLOCK, 2 * KV_WIDTH),
                         lambda b, n: (b * nb + jnp.maximum(n - 1, 0), kv_col)),
            pl.BlockSpec((BLOCK, 2 * KV_WIDTH), lambda b, n: (b * nb + n, kv_col)),
            pl.BlockSpec((1, 2 * N_KV_HEADS, CHUNKS_PER_KV * BLOCK, 2 * BLOCK),
                         lambda b, n: (jnp.minimum(n, 1), 0, 0, 0)),
            pl.BlockSpec((1, ATTN_WIDTH), lambda b, n: (0, 0)),
        ],
        out_specs=pl.BlockSpec((BLOCK, ATTN_WIDTH), lambda b, n: (b * nb + n, 0)),
        scratch_shapes=[pltpu.VMEM((BLOCK, ATTN_WIDTH), jnp.float32)],
        compiler_params=pltpu.CompilerParams(
            dimension_semantics=("arbitrary", "arbitrary"), vmem_limit_bytes=VMEM_LIMIT),
        name="swa_attention",
    )(main, main, main, bias, gain)


def _attention_bias(sinks):
    q_pos = np.arange(BLOCK) + BLOCK
    k_pos = np.arange(2 * BLOCK)
    dist = (q_pos[:, None] - k_pos[None, :]).astype(np.float32)
    band = (dist >= 0) & (dist < BLOCK)
    assert not band[:, 0].any()
    valid = np.stack([band & (k_pos >= BLOCK)[None, :], band])
    slopes = jnp.exp2(-8.0 * jnp.arange(1, N_Q_HEADS + 1, dtype=jnp.float32) / N_Q_HEADS)
    alibi = -slopes[None, :, None, None] * jnp.asarray(dist)[None, None]
    bias = jnp.where(jnp.asarray(valid)[:, None], alibi, NEG_BIG)
    sink_slot = jnp.asarray(k_pos == 0)[None, None, None, :]
    bias = jnp.where(sink_slot, sinks.astype(jnp.float32)[None, :, None, None], bias)
    bias = bias.reshape(2, N_KV_HEADS, CHUNKS_PER_KV, 2, BLOCK, 2 * BLOCK)
    return bias.transpose(0, 1, 3, 2, 4, 5).reshape(
        2, 2 * N_KV_HEADS, CHUNKS_PER_KV * BLOCK, 2 * BLOCK)


def _hgrn_tables():
    c = CHUNK
    rows = []
    masks = []
    t = np.arange(c)
    for lvl in range(N_LEVELS):
        h = c >> (lvl + 1)
        odd = (t // h) % 2 == 1
        m = np.zeros((c, c), np.float32)
        for r in range(c):
            if odd[r]:
                m[r, (r // h) * h:r + 1] = 1.0
            else:
                m[r, r + 1:(r // h + 1) * h] = 1.0
        rows.append(m)
        same = (t[:, None] // (2 * h)) == (t[None, :] // (2 * h))
        masks.append((same & odd[:, None] & (~odd)[None, :]).astype(np.float32))
    rows.append(np.tril(np.ones((c, c), np.float32)))
    rows.append(np.triu(np.ones((c, c), np.float32), 1))
    masks.append(np.eye(c, dtype=np.float32))
    mall = np.concatenate(rows, axis=0)
    masks = np.stack(masks)
    return np.concatenate([mall, mall], axis=1), np.concatenate([masks, masks], axis=2)


def _hgrn_kernel(lbl_ref, q_ref, i_ref, g_ref, f_ref, mall_ref, mask_ref, gain_ref,
                 o_ref, st_ref, *, n_chunks):
    @pl.when(pl.program_id(1) == 0)
    def _():
        st_ref[...] = jnp.zeros_like(st_ref)

    lbl = lbl_ref[...]
    ex = jnp.exp(lbl - jnp.max(lbl, axis=0, keepdims=True))
    lb = ex[0:1] / jnp.sum(ex, axis=0, keepdims=True)
    gain = gain_ref[...]
    d, c = RNN_HEAD_DIM, CHUNK
    bf16 = jnp.bfloat16
    zrow = jnp.zeros((c, d), bf16)
    zst = jnp.zeros((d, d), bf16)

    def blockdiag(a, b, z):
        return jnp.concatenate([jnp.concatenate([a, z], axis=1),
                                jnp.concatenate([z, b], axis=1)], axis=0)

    def chunk(ci, carry):
        r0 = pl.multiple_of(ci * c, c)
        rows = pl.ds(r0, c)
        fl = f_ref[rows, :]
        f = lb + (1.0 - lb) * (1.0 / (1.0 + jnp.exp(-fl)))
        lf = jnp.log(f)
        kb = (1.0 - f).astype(bf16)
        hi = lf.astype(bf16)
        lo = (lf - hi.astype(jnp.float32)).astype(bf16)
        x = _dot(mall_ref[...], jnp.concatenate([hi, lo], axis=0))
        e = jnp.exp(x).astype(bf16)
        decay = jnp.exp(x[(N_LEVELS + 1) * c - 1:(N_LEVELS + 1) * c, :])
        qx = q_ref[rows, :].astype(jnp.float32)
        qf = (qx * (1.0 / (1.0 + jnp.exp(-qx)))).astype(bf16)
        v = i_ref[rows, :]
        lvl = lambda l: e[l * c:(l + 1) * c]
        ql = [qf * lvl(l) for l in range(N_LEVELS)] + [qf]
        kl = [kb * lvl(l) for l in range(N_LEVELS)] + [kb]
        qb = qf * lvl(N_LEVELS)
        kr = kb * lvl(N_LEVELS + 1)
        outs = []
        for p in range(N_RNN_HEADS // 2):
            c0, c1, c2 = 2 * p * d, (2 * p + 1) * d, (2 * p + 2) * d
            att = None
            for l in range(N_LEVELS + 1):
                rhs = blockdiag(kl[l][:, c0:c1], kl[l][:, c1:c2], zrow)
                r = mask_ref[l] * _dot_nt(ql[l][:, c0:c2], rhs)
                att = r if att is None else att + r
            st0, st1 = st_ref[2 * p], st_ref[2 * p + 1]
            o = (_dot(att.astype(bf16), blockdiag(v[:, c0:c1], v[:, c1:c2], zrow))
                 + _dot_nt(qb[:, c0:c2], blockdiag(st0.astype(bf16), st1.astype(bf16), zst)))
            outs.append(o)
            st_ref[2 * p] = st0 * decay[:, c0:c1] + _dot_tn(v[:, c0:c1], kr[:, c0:c1])
            st_ref[2 * p + 1] = st1 * decay[:, c1:c2] + _dot_tn(v[:, c1:c2], kr[:, c1:c2])
        scales = []
        for hd in range(N_RNN_HEADS):
            oh = outs[hd // 2][:, (hd % 2) * d:(hd % 2 + 1) * d]
            scales.append(jnp.broadcast_to(_rms_scale(oh, d), (c, d)))
        gx = g_ref[rows, :].astype(jnp.float32)
        gate = gx * (1.0 / (1.0 + jnp.exp(-gx)))
        o_all = jnp.concatenate(outs, axis=1) * jnp.concatenate(scales, axis=1)
        o_ref[rows, :] = (o_all * gain * gate).astype(bf16)
        return carry

    lax.fori_loop(0, n_chunks, chunk, 0, unroll=4)


def _hgrn(main, f_r, lb_logits, gain, batch, seq, ts):
    mall, masks = _hgrn_tables()
    mall = jnp.asarray(mall, jnp.bfloat16)
    masks = jnp.asarray(masks)
    ns = seq // ts
    blk = lambda col: pl.BlockSpec((ts, RNN_WIDTH), lambda b, s: (b * ns + s, col))
    const = lambda shape: pl.BlockSpec(shape, lambda b, s: (0,) * len(shape))
    return pl.pallas_call(
        functools.partial(_hgrn_kernel, n_chunks=ts // CHUNK),
        out_shape=jax.ShapeDtypeStruct((batch * seq, RNN_WIDTH), jnp.bfloat16),
        grid=(batch, ns),
        in_specs=[
            const(lb_logits.shape),
            blk(1), blk(2), blk(3),
            pl.BlockSpec((ts, RNN_WIDTH), lambda b, s: (b * ns + s, 0)),
            const(mall.shape), const(masks.shape), const(gain.shape),
        ],
        out_specs=pl.BlockSpec((ts, RNN_WIDTH), lambda b, s: (b * ns + s, 0)),
        scratch_shapes=[pltpu.VMEM((N_RNN_HEADS, RNN_HEAD_DIM, RNN_HEAD_DIM), jnp.float32)],
        compiler_params=pltpu.CompilerParams(
            dimension_semantics=("arbitrary", "arbitrary"), vmem_limit_bytes=VMEM_LIMIT),
        name="hgrn2",
    )(lb_logits, main, main, main, f_r, mall, masks, gain)


def _outproj_kernel(a_ref, r_ref, x_ref, w_ref, post_ref, pre_ref, x1_ref, h_ref):
    mixed = _dot(a_ref[...], w_ref[:ATTN_WIDTH, :]) + _dot(r_ref[...], w_ref[ATTN_WIDTH:, :])
    x1 = x_ref[...] + mixed * _rms_scale(mixed, D_MODEL) * post_ref[...]
    x1_ref[...] = x1
    h_ref[...] = (x1 * _rms_scale(x1, D_MODEL) * pre_ref[...]).astype(jnp.bfloat16)


def _outproj(attn, rnn, x2, w_out, post_gain, pre_gain, tm):
    t = x2.shape[0]
    row = lambda w: pl.BlockSpec((tm, w), lambda i: (i, 0))
    const = lambda shape, **kw: pl.BlockSpec(shape, lambda i: (0, 0), **kw)
    return pl.pallas_call(
        _outproj_kernel,
        out_shape=(jax.ShapeDtypeStruct((t, D_MODEL), jnp.float32),
                   jax.ShapeDtypeStruct((t, D_MODEL), jnp.bfloat16)),
        grid=(t // tm,),
        in_specs=[row(ATTN_WIDTH), row(RNN_WIDTH), row(D_MODEL),
                  const(w_out.shape, pipeline_mode=pl.Buffered(1)),
                  const((1, D_MODEL)), const((1, D_MODEL))],
        out_specs=(row(D_MODEL), row(D_MODEL)),
        compiler_params=pltpu.CompilerParams(
            dimension_semantics=("arbitrary",), vmem_limit_bytes=VMEM_LIMIT),
        name="outproj",
    )(attn, rnn, x2, w_out, post_gain, pre_gain)


def _mlp_kernel(h_ref, wu_ref, wd_ref, x1_ref, gain_ref, o_ref, acc_ref):
    f = pl.program_id(1)

    @pl.when(f == 0)
    def _():
        acc_ref[...] = jnp.zeros_like(acc_ref)

    u = jnp.maximum(_dot(h_ref[...], wu_ref[...]), 0.0)
    acc_ref[...] += _dot((u * u).astype(jnp.bfloat16), wd_ref[...])

    @pl.when(f == pl.num_programs(1) - 1)
    def _():
        y = acc_ref[...]
        o_ref[...] = x1_ref[...] + y * _rms_scale(y, D_MODEL) * gain_ref[...]


def _mlp(h2, w_up, w_down, x1, gain, tm, tf):
    t = h2.shape[0]
    return pl.pallas_call(
        _mlp_kernel,
        out_shape=jax.ShapeDtypeStruct((t, D_MODEL), jnp.float32),
        grid=(t // tm, D_FF // tf),
        in_specs=[
            pl.BlockSpec((tm, D_MODEL), lambda i, f: (i, 0)),
            pl.BlockSpec((D_MODEL, tf), lambda i, f: (0, f)),
            pl.BlockSpec((tf, D_MODEL), lambda i, f: (f, 0)),
            pl.BlockSpec((tm, D_MODEL), lambda i, f: (i, 0)),
            pl.BlockSpec((1, D_MODEL), lambda i, f: (0, 0)),
        ],
        out_specs=pl.BlockSpec((tm, D_MODEL), lambda i, f: (i, 0)),
        scratch_shapes=[pltpu.VMEM((tm, D_MODEL), jnp.float32)],
        compiler_params=pltpu.CompilerParams(
            dimension_semantics=("arbitrary", "arbitrary"), vmem_limit_bytes=VMEM_LIMIT),
        name="mlp",
    )(h2, w_up, w_down, x1, gain)


def kernel(x, w_in, attn_sinks, attn_out_gain, rnn_lb_logits, rnn_norm_gain, w_out,
           mix_pre_gain, mix_post_gain, mlp_pre_gain, mlp_post_gain, w_up, w_down):
    batch, seq, _ = x.shape
    layer = 0
    x2 = x.reshape(batch * seq, D_MODEL)
    a, kv, r = ATTN_WIDTH, 2 * KV_WIDTH, RNN_WIDTH
    w = w_in[layer]
    w_perm = jnp.concatenate(
        [w[:, :a], w[:, a + kv:a + kv + r], w[:, a + kv + 2 * r:a + kv + 3 * r],
         w[:, a + kv + 3 * r:], w[:, a:a + kv], w[:, a + kv + r:a + kv + 2 * r]],
        axis=1).astype(jnp.bfloat16)

    main, f_r = _inproj(x2, mix_pre_gain[layer][None], w_perm, tm=512)
    attn = _attention(main, _attention_bias(attn_sinks[layer]), attn_out_gain[layer][None], batch, seq)
    rnn = _hgrn(main, f_r, rnn_lb_logits, jnp.tile(rnn_norm_gain[layer], N_RNN_HEADS)[None],
                batch, seq, ts=512)
    x1, h2 = _outproj(attn, rnn, x2, w_out[layer].astype(jnp.bfloat16),
                      mix_post_gain[layer][None], mlp_pre_gain[layer][None], tm=512)
    out = _mlp(h2, w_up[layer].astype(jnp.bfloat16), w_down[layer].astype(jnp.bfloat16),
               x1, mlp_post_gain[layer][None], tm=512, tf=1024)
    return out.reshape(batch, seq, D_MODEL)
```

```python
import functools

import numpy as np
import jax
import jax.numpy as jnp
from jax import lax
from jax.experimental import pallas as pl
from jax.experimental.pallas import tpu as pltpu

D_MODEL = 2048
ATTN_WIDTH = 1024
HEAD_DIM = 64
N_Q_HEADS = 16
N_KV_HEADS = 2
KV_WIDTH = 128
BLOCK = 128
RNN_WIDTH = 1024
RNN_HEAD_DIM = 128
N_RNN_HEADS = 8
CHUNK = 64
D_FF = 8192
EPS = 1e-6

LANES = 128
CHUNKS_PER_KV = ATTN_WIDTH // LANES // N_KV_HEADS
VMEM_LIMIT = 56 * 1024 * 1024

MAIN_WIDTH = ATTN_WIDTH + 3 * RNN_WIDTH + 2 * KV_WIDTH
_KV_SRC = ATTN_WIDTH
_QR_SRC = _KV_SRC + 2 * KV_WIDTH
_F_SRC = _QR_SRC + RNN_WIDTH
_I_SRC = _F_SRC + RNN_WIDTH
_G_SRC = _I_SRC + RNN_WIDTH
_MAIN_PIECES = ((0, ATTN_WIDTH), (_QR_SRC, RNN_WIDTH), (_I_SRC, RNN_WIDTH), (_G_SRC, RNN_WIDTH),
                (_KV_SRC, 2 * KV_WIDTH))
N_LEVELS = 6
NEG_BIG = -1e30

_NT = (((1,), (1,)), ((), ()))
_TN = (((0,), (0,)), ((), ()))


def _dot(a, b):
    return jnp.dot(a, b, preferred_element_type=jnp.float32)


def _dot_nt(a, b):
    return lax.dot_general(a, b, _NT, preferred_element_type=jnp.float32)


def _dot_tn(a, b):
    return lax.dot_general(a, b, _TN, preferred_element_type=jnp.float32)


def _rms_scale(y, width):
    return lax.rsqrt(jnp.sum(y * y, axis=-1, keepdims=True) * (1.0 / width) + EPS)


def _inproj_kernel(x_ref, gain_ref, w_ref, main_ref, f_ref):
    x = x_ref[...]
    h = (x * _rms_scale(x, D_MODEL) * gain_ref[...]).astype(jnp.bfloat16)
    dst = 0
    for src, wdt in _MAIN_PIECES:
        main_ref[:, dst:dst + wdt] = _dot(h, w_ref[:, src:src + wdt]).astype(jnp.bfloat16)
        dst += wdt
    f_ref[...] = _dot(h, w_ref[:, _F_SRC:_F_SRC + RNN_WIDTH])


def _inproj(x2, gain, w, tm):
    t = x2.shape[0]
    n_all = w.shape[1]
    return pl.pallas_call(
        _inproj_kernel,
        out_shape=(jax.ShapeDtypeStruct((t, MAIN_WIDTH), jnp.bfloat16),
                   jax.ShapeDtypeStruct((t, RNN_WIDTH), jnp.float32)),
        grid=(t // tm,),
        in_specs=[
            pl.BlockSpec((tm, D_MODEL), lambda i: (i, 0)),
            pl.BlockSpec((1, D_MODEL), lambda i: (0, 0)),
            pl.BlockSpec((D_MODEL, n_all), lambda i: (0, 0), pipeline_mode=pl.Buffered(1)),
        ],
        out_specs=(pl.BlockSpec((tm, MAIN_WIDTH), lambda i: (i, 0)),
                   pl.BlockSpec((tm, RNN_WIDTH), lambda i: (i, 0))),
        compiler_params=pltpu.CompilerParams(
            dimension_semantics=("arbitrary",), vmem_limit_bytes=VMEM_LIMIT),
        name="inproj",
    )(x2, gain, w)


def _attn_kernel(q_ref, kvp_ref, kvc_ref, bias0_ref, bias_ref, gain_ref, o_ref, acc_ref, *, nsub):
    h = HEAD_DIM
    kv_all = jnp.concatenate([kvp_ref[...], kvc_ref[...]], axis=0)
    lane = lax.broadcasted_iota(jnp.int32, (2 * BLOCK, 4 * h), 1)
    row = lax.broadcasted_iota(jnp.int32, (2 * BLOCK, 4 * h), 0)
    lo = ((lane % LANES) < h) & (row > 0)
    hi = ((lane % LANES) >= h) & (row > 0)
    zero = jnp.zeros((2 * BLOCK, 4 * h), jnp.bfloat16)
    ones = jnp.ones((2 * BLOCK, LANES), jnp.bfloat16)
    for sub in range(nsub):
        rows = slice(sub * BLOCK, (sub + 1) * BLOCK)
        bias = bias0_ref if sub == 0 else bias_ref
        q = q_ref[rows, :] * jnp.bfloat16(h ** -0.5)
        kv = kv_all[sub * BLOCK:(sub + 2) * BLOCK]
        kv_sw = jnp.concatenate([kv[:, h:2 * h], kv[:, 0:h], kv[:, 3 * h:4 * h], kv[:, 2 * h:3 * h]],
                                axis=1)
        placed = {
            (0, 0): jnp.where(lo, kv, zero), (0, 1): jnp.where(hi, kv_sw, zero),
            (1, 0): jnp.where(lo, kv_sw, zero), (1, 1): jnp.where(hi, kv, zero),
        }
        sq = jnp.zeros((BLOCK, LANES), jnp.float32)
        for kvh in range(N_KV_HEADS):
            chunks = [CHUNKS_PER_KV * kvh + j for j in range(CHUNKS_PER_KV)]
            q_stack = jnp.concatenate([q[:, c * LANES:(c + 1) * LANES] for c in chunks], axis=0)
            out = None
            for par in range(2):
                kx = placed[(kvh, par)][:, :LANES]
                vx = jnp.concatenate([placed[(kvh, par)][:, LANES:], ones], axis=1)
                s = _dot_nt(q_stack, kx) + bias[0, 2 * kvh + par]
                s0, s1 = s[:, :BLOCK], s[:, BLOCK:]
                m = jnp.max(jnp.maximum(s0, s1), axis=1, keepdims=True)
                m = jnp.broadcast_to(m, s0.shape)
                p = jnp.concatenate([jnp.exp(s0 - m), jnp.exp(s1 - m)], axis=1).astype(jnp.bfloat16)
                r = _dot(p, vx)
                contrib = r[:, :LANES] * (1.0 / r[:, LANES:])
                out = contrib if out is None else out + contrib
            for j, c in enumerate(chunks):
                oc = out[j * BLOCK:(j + 1) * BLOCK]
                acc_ref[rows, c * LANES:(c + 1) * LANES] = oc
                sq = sq + oc * oc
        scale = lax.rsqrt(jnp.sum(sq, axis=1, keepdims=True) * (1.0 / ATTN_WIDTH) + EPS)
        o_ref[rows, :] = (acc_ref[rows, :] * scale * gain_ref[...]).astype(jnp.bfloat16)


def _attention(main, bias, gain, batch, seq, nsub):
    nb = seq // BLOCK
    ns = nb // nsub
    kv_col = (ATTN_WIDTH + 3 * RNN_WIDTH) // (2 * KV_WIDTH)
    bias_shape = (1,) + bias.shape[1:]
    return pl.pallas_call(
        functools.partial(_attn_kernel, nsub=nsub),
        out_shape=jax.ShapeDtypeStruct((batch * seq, ATTN_WIDTH), jnp.bfloat16),
        grid=(batch, ns),
        in_specs=[
            pl.BlockSpec((nsub * BLOCK, ATTN_WIDTH), lambda b, n: (b * ns + n, 0)),
            pl.BlockSpec((BLOCK, 2 * KV_WIDTH),
                         lambda b, n: (b * nb + jnp.maximum(nsub * n - 1, 0), kv_col)),
            pl.BlockSpec((nsub * BLOCK, 2 * KV_WIDTH), lambda b, n: (b * ns + n, kv_col)),
            pl.BlockSpec(bias_shape, lambda b, n: (jnp.minimum(n, 1), 0, 0, 0)),
            pl.BlockSpec(bias_shape, lambda b, n: (1, 0, 0, 0)),
            pl.BlockSpec((1, ATTN_WIDTH), lambda b, n: (0, 0)),
        ],
        out_specs=pl.BlockSpec((nsub * BLOCK, ATTN_WIDTH), lambda b, n: (b * ns + n, 0)),
        scratch_shapes=[pltpu.VMEM((nsub * BLOCK, ATTN_WIDTH), jnp.float32)],
        compiler_params=pltpu.CompilerParams(
            dimension_semantics=("arbitrary", "arbitrary"), vmem_limit_bytes=VMEM_LIMIT),
        name="swa_attention",
    )(main, main, main, bias, bias, gain)


def _attention_bias(sinks):
    q_pos = np.arange(BLOCK) + BLOCK
    k_pos = np.arange(2 * BLOCK)
    dist = (q_pos[:, None] - k_pos[None, :]).astype(np.float32)
    band = (dist >= 0) & (dist < BLOCK)
    assert not band[:, 0].any()
    valid = np.stack([band & (k_pos >= BLOCK)[None, :], band])
    slopes = jnp.exp2(-8.0 * jnp.arange(1, N_Q_HEADS + 1, dtype=jnp.float32) / N_Q_HEADS)
    alibi = -slopes[None, :, None, None] * jnp.asarray(dist)[None, None]
    bias = jnp.where(jnp.asarray(valid)[:, None], alibi, NEG_BIG)
    sink_slot = jnp.asarray(k_pos == 0)[None, None, None, :]
    bias = jnp.where(sink_slot, sinks.astype(jnp.float32)[None, :, None, None], bias)
    bias = bias.reshape(2, N_KV_HEADS, CHUNKS_PER_KV, 2, BLOCK, 2 * BLOCK)
    return bias.transpose(0, 1, 3, 2, 4, 5).reshape(
        2, 2 * N_KV_HEADS, CHUNKS_PER_KV * BLOCK, 2 * BLOCK)


def _hgrn_tables():
    c = CHUNK
    rows = []
    masks = []
    t = np.arange(c)
    for lvl in range(N_LEVELS):
        h = c >> (lvl + 1)
        odd = (t // h) % 2 == 1
        m = np.zeros((c, c), np.float32)
        for r in range(c):
            if odd[r]:
                m[r, (r // h) * h:r + 1] = 1.0
            else:
                m[r, r + 1:(r // h + 1) * h] = 1.0
        rows.append(m)
        same = (t[:, None] // (2 * h)) == (t[None, :] // (2 * h))
        masks.append((same & odd[:, None] & (~odd)[None, :]).astype(np.float32))
    rows.append(np.tril(np.ones((c, c), np.float32)))
    rows.append(np.triu(np.ones((c, c), np.float32), 1))
    masks.append(np.eye(c, dtype=np.float32))
    mall = np.concatenate(rows, axis=0)
    masks = np.stack(masks)
    return np.concatenate([mall, mall], axis=1), np.concatenate([masks, masks], axis=2)


def _hgrn_kernel(lbl_ref, q_ref, i_ref, g_ref, f_ref, mall_ref, mask_ref, gain_ref,
                 o_ref, st_ref, *, n_chunks):
    @pl.when(pl.program_id(1) == 0)
    def _():
        st_ref[...] = jnp.zeros_like(st_ref)

    lbl = lbl_ref[...]
    ex = jnp.exp(lbl - jnp.max(lbl, axis=0, keepdims=True))
    lb = ex[0:1] / jnp.sum(ex, axis=0, keepdims=True)
    gain = gain_ref[...]
    d, c = RNN_HEAD_DIM, CHUNK
    bf16 = jnp.bfloat16
    zrow = jnp.zeros((c, d), bf16)
    zst = jnp.zeros((d, d), bf16)

    def blockdiag(a, b, z):
        return jnp.concatenate([jnp.concatenate([a, z], axis=1),
                                jnp.concatenate([z, b], axis=1)], axis=0)

    def chunk(ci, carry):
        r0 = pl.multiple_of(ci * c, c)
        rows = pl.ds(r0, c)
        fl = f_ref[rows, :]
        f = lb + (1.0 - lb) * (1.0 / (1.0 + jnp.exp(-fl)))
        lf = jnp.log(f)
        kb = (1.0 - f).astype(bf16)
        hi = lf.astype(bf16)
        lo = (lf - hi.astype(jnp.float32)).astype(bf16)
        x = _dot(mall_ref[...], jnp.concatenate([hi, lo], axis=0))
        e = jnp.exp(x).astype(bf16)
        decay = jnp.exp(x[(N_LEVELS + 1) * c - 1:(N_LEVELS + 1) * c, :])
        qx = q_ref[rows, :].astype(jnp.float32)
        qf = (qx * (1.0 / (1.0 + jnp.exp(-qx)))).astype(bf16)
        v = i_ref[rows, :]
        lvl = lambda l: e[l * c:(l + 1) * c]
        ql = [qf * lvl(l) for l in range(N_LEVELS)] + [qf]
        kl = [kb * lvl(l) for l in range(N_LEVELS)] + [kb]
        qb = qf * lvl(N_LEVELS)
        kr = kb * lvl(N_LEVELS + 1)
        outs = []
        for p in range(N_RNN_HEADS // 2):
            c0, c1, c2 = 2 * p * d, (2 * p + 1) * d, (2 * p + 2) * d
            att = None
            for l in range(N_LEVELS + 1):
                rhs = blockdiag(kl[l][:, c0:c1], kl[l][:, c1:c2], zrow)
                r = mask_ref[l] * _dot_nt(ql[l][:, c0:c2], rhs)
                att = r if att is None else att + r
            st0, st1 = st_ref[2 * p], st_ref[2 * p + 1]
            o = (_dot(att.astype(bf16), blockdiag(v[:, c0:c1], v[:, c1:c2], zrow))
                 + _dot_nt(qb[:, c0:c2], blockdiag(st0.astype(bf16), st1.astype(bf16), zst)))
            outs.append(o)
            st_ref[2 * p] = st0 * decay[:, c0:c1] + _dot_tn(v[:, c0:c1], kr[:, c0:c1])
            st_ref[2 * p + 1] = st1 * decay[:, c1:c2] + _dot_tn(v[:, c1:c2], kr[:, c1:c2])
        scales = []
        for hd in range(N_RNN_HEADS):
            oh = outs[hd // 2][:, (hd % 2) * d:(hd % 2 + 1) * d]
            scales.append(jnp.broadcast_to(_rms_scale(oh, d), (c, d)))
        gx = g_ref[rows, :].astype(jnp.float32)
        gate = gx * (1.0 / (1.0 + jnp.exp(-gx)))
        o_all = jnp.concatenate(outs, axis=1) * jnp.concatenate(scales, axis=1)
        o_ref[rows, :] = (o_all * gain * gate).astype(bf16)
        return carry

    lax.fori_loop(0, n_chunks, chunk, 0, unroll=4)


def _hgrn(main, f_r, lb_logits, gain, batch, seq, ts):
    mall, masks = _hgrn_tables()
    mall = jnp.asarray(mall, jnp.bfloat16)
    masks = jnp.asarray(masks)
    ns = seq // ts
    blk = lambda col: pl.BlockSpec((ts, RNN_WIDTH), lambda b, s: (b * ns + s, col))
    const = lambda shape: pl.BlockSpec(shape, lambda b, s: (0,) * len(shape))
    return pl.pallas_call(
        functools.partial(_hgrn_kernel, n_chunks=ts // CHUNK),
        out_shape=jax.ShapeDtypeStruct((batch * seq, RNN_WIDTH), jnp.bfloat16),
        grid=(batch, ns),
        in_specs=[
            const(lb_logits.shape),
            blk(1), blk(2), blk(3),
            pl.BlockSpec((ts, RNN_WIDTH), lambda b, s: (b * ns + s, 0)),
            const(mall.shape), const(masks.shape), const(gain.shape),
        ],
        out_specs=pl.BlockSpec((ts, RNN_WIDTH), lambda b, s: (b * ns + s, 0)),
        scratch_shapes=[pltpu.VMEM((N_RNN_HEADS, RNN_HEAD_DIM, RNN_HEAD_DIM), jnp.float32)],
        compiler_params=pltpu.CompilerParams(
            dimension_semantics=("arbitrary", "arbitrary"), vmem_limit_bytes=VMEM_LIMIT),
        name="hgrn2",
    )(lb_logits, main, main, main, f_r, mall, masks, gain)


def _outproj_kernel(a_ref, r_ref, x_ref, w_ref, post_ref, pre_ref, x1_ref, h_ref):
    mixed = _dot(a_ref[...], w_ref[:ATTN_WIDTH, :]) + _dot(r_ref[...], w_ref[ATTN_WIDTH:, :])
    x1 = x_ref[...] + mixed * _rms_scale(mixed, D_MODEL) * post_ref[...]
    x1_ref[...] = x1
    h_ref[...] = (x1 * _rms_scale(x1, D_MODEL) * pre_ref[...]).astype(jnp.bfloat16)


def _outproj(attn, rnn, x2, w_out, post_gain, pre_gain, tm):
    t = x2.shape[0]
    row = lambda w: pl.BlockSpec((tm, w), lambda i: (i, 0))
    const = lambda shape, **kw: pl.BlockSpec(shape, lambda i: (0, 0), **kw)
    return pl.pallas_call(
        _outproj_kernel,
        out_shape=(jax.ShapeDtypeStruct((t, D_MODEL), jnp.float32),
                   jax.ShapeDtypeStruct((t, D_MODEL), jnp.bfloat16)),
        grid=(t // tm,),
        in_specs=[row(ATTN_WIDTH), row(RNN_WIDTH), row(D_MODEL),
                  const(w_out.shape, pipeline_mode=pl.Buffered(1)),
                  const((1, D_MODEL)), const((1, D_MODEL))],
        out_specs=(row(D_MODEL), row(D_MODEL)),
        compiler_params=pltpu.CompilerParams(
            dimension_semantics=("arbitrary",), vmem_limit_bytes=VMEM_LIMIT),
        name="outproj",
    )(attn, rnn, x2, w_out, post_gain, pre_gain)


def _mlp_kernel(h_ref, wu_ref, wd_ref, x1_ref, gain_ref, o_ref, acc_ref):
    f = pl.program_id(1)

    @pl.when(f == 0)
    def _():
        acc_ref[...] = jnp.zeros_like(acc_ref)

    u = jnp.maximum(_dot(h_ref[...], wu_ref[...]), 0.0)
    acc_ref[...] += _dot((u * u).astype(jnp.bfloat16), wd_ref[...])

    @pl.when(f == pl.num_programs(1) - 1)
    def _():
        y = acc_ref[...]
        o_ref[...] = x1_ref[...] + y * _rms_scale(y, D_MODEL) * gain_ref[...]


def _mlp(h2, w_up, w_down, x1, gain, tm, tf):
    t = h2.shape[0]
    return pl.pallas_call(
        _mlp_kernel,
        out_shape=jax.ShapeDtypeStruct((t, D_MODEL), jnp.float32),
        grid=(t // tm, D_FF // tf),
        in_specs=[
            pl.BlockSpec((tm, D_MODEL), lambda i, f: (i, 0)),
            pl.BlockSpec((D_MODEL, tf), lambda i, f: (0, f)),
            pl.BlockSpec((tf, D_MODEL), lambda i, f: (f, 0)),
            pl.BlockSpec((tm, D_MODEL), lambda i, f: (i, 0)),
            pl.BlockSpec((1, D_MODEL), lambda i, f: (0, 0)),
        ],
        out_specs=pl.BlockSpec((tm, D_MODEL), lambda i, f: (i, 0)),
        scratch_shapes=[pltpu.VMEM((tm, D_MODEL), jnp.float32)],
        compiler_params=pltpu.CompilerParams(
            dimension_semantics=("arbitrary", "arbitrary"), vmem_limit_bytes=VMEM_LIMIT),
        name="mlp",
    )(h2, w_up, w_down, x1, gain)


def kernel(x, w_in, attn_sinks, attn_out_gain, rnn_lb_logits, rnn_norm_gain, w_out,
           mix_pre_gain, mix_post_gain, mlp_pre_gain, mlp_post_gain, w_up, w_down):
    batch, seq, _ = x.shape
    layer = 0
    x2 = x.reshape(batch * seq, D_MODEL)
    main, f_r = _inproj(x2, mix_pre_gain[layer][None], w_in[layer].astype(jnp.bfloat16), tm=512)
    attn = _attention(main, _attention_bias(attn_sinks[layer]), attn_out_gain[layer][None],
                      batch, seq, nsub=4)
    rnn = _hgrn(main, f_r, rnn_lb_logits, jnp.tile(rnn_norm_gain[layer], N_RNN_HEADS)[None],
                batch, seq, ts=512)
    x1, h2 = _outproj(attn, rnn, x2, w_out[layer].astype(jnp.bfloat16),
                      mix_post_gain[layer][None], mlp_pre_gain[layer][None], tm=512)
    out = _mlp(h2, w_up[layer].astype(jnp.bfloat16), w_down[layer].astype(jnp.bfloat16),
               x1, mlp_post_gain[layer][None], tm=512, tf=1024)
    return out.reshape(batch, seq, D_MODEL)
```

```python
import functools

import numpy as np
import jax
import jax.numpy as jnp
from jax import lax
from jax.experimental import pallas as pl
from jax.experimental.pallas import tpu as pltpu

D_MODEL = 2048
ATTN_WIDTH = 1024
HEAD_DIM = 64
N_Q_HEADS = 16
N_KV_HEADS = 2
KV_WIDTH = 128
BLOCK = 128
RNN_WIDTH = 1024
RNN_HEAD_DIM = 128
N_RNN_HEADS = 8
CHUNK = 64
D_FF = 8192
EPS = 1e-6

LANES = 128
CHUNKS_PER_KV = ATTN_WIDTH // LANES // N_KV_HEADS
VMEM_LIMIT = 56 * 1024 * 1024

MAIN_WIDTH = ATTN_WIDTH + 3 * RNN_WIDTH + 2 * KV_WIDTH
_KV_SRC = ATTN_WIDTH
_QR_SRC = _KV_SRC + 2 * KV_WIDTH
_F_SRC = _QR_SRC + RNN_WIDTH
_I_SRC = _F_SRC + RNN_WIDTH
_G_SRC = _I_SRC + RNN_WIDTH
_MAIN_PIECES = ((0, ATTN_WIDTH), (_QR_SRC, RNN_WIDTH), (_I_SRC, RNN_WIDTH), (_G_SRC, RNN_WIDTH),
                (_KV_SRC, 2 * KV_WIDTH))
N_LEVELS = 6
NEG_BIG = -1e30

_NT = (((1,), (1,)), ((), ()))
_TN = (((0,), (0,)), ((), ()))


def _dot(a, b):
    return jnp.dot(a, b, preferred_element_type=jnp.float32)


def _dot_nt(a, b):
    return lax.dot_general(a, b, _NT, preferred_element_type=jnp.float32)


def _dot_tn(a, b):
    return lax.dot_general(a, b, _TN, preferred_element_type=jnp.float32)


def _rms_scale(y, width):
    return lax.rsqrt(jnp.sum(y * y, axis=-1, keepdims=True) * (1.0 / width) + EPS)


def _inproj_kernel(x_ref, gain_ref, w_ref, main_ref, f_ref):
    x = x_ref[...]
    h = (x * _rms_scale(x, D_MODEL) * gain_ref[...]).astype(jnp.bfloat16)
    dst = 0
    for src, wdt in _MAIN_PIECES:
        main_ref[:, dst:dst + wdt] = _dot(h, w_ref[:, src:src + wdt]).astype(jnp.bfloat16)
        dst += wdt
    f_ref[...] = _dot(h, w_ref[:, _F_SRC:_F_SRC + RNN_WIDTH])


def _inproj(x2, gain, w, tm):
    t = x2.shape[0]
    n_all = w.shape[1]
    return pl.pallas_call(
        _inproj_kernel,
        out_shape=(jax.ShapeDtypeStruct((t, MAIN_WIDTH), jnp.bfloat16),
                   jax.ShapeDtypeStruct((t, RNN_WIDTH), jnp.float32)),
        grid=(t // tm,),
        in_specs=[
            pl.BlockSpec((tm, D_MODEL), lambda i: (i, 0)),
            pl.BlockSpec((1, D_MODEL), lambda i: (0, 0)),
            pl.BlockSpec((D_MODEL, n_all), lambda i: (0, 0), pipeline_mode=pl.Buffered(1)),
        ],
        out_specs=(pl.BlockSpec((tm, MAIN_WIDTH), lambda i: (i, 0)),
                   pl.BlockSpec((tm, RNN_WIDTH), lambda i: (i, 0))),
        compiler_params=pltpu.CompilerParams(
            dimension_semantics=("arbitrary",), vmem_limit_bytes=VMEM_LIMIT),
        name="inproj",
    )(x2, gain, w)


def _attn_kernel(q_ref, kvp_ref, kvc_ref, bias0_ref, bias_ref, gain_ref, o_ref, acc_ref, *, nsub):
    h = HEAD_DIM
    kv_all = jnp.concatenate([kvp_ref[...], kvc_ref[...]], axis=0)
    lane = lax.broadcasted_iota(jnp.int32, (2 * BLOCK, 4 * h), 1)
    row = lax.broadcasted_iota(jnp.int32, (2 * BLOCK, 4 * h), 0)
    lo = ((lane % LANES) < h) & (row > 0)
    hi = ((lane % LANES) >= h) & (row > 0)
    zero = jnp.zeros((2 * BLOCK, 4 * h), jnp.bfloat16)
    ones = jnp.ones((2 * BLOCK, LANES), jnp.bfloat16)
    for sub in range(nsub):
        rows = slice(sub * BLOCK, (sub + 1) * BLOCK)
        bias = bias0_ref if sub == 0 else bias_ref
        q = q_ref[rows, :] * jnp.bfloat16(h ** -0.5)
        kv = kv_all[sub * BLOCK:(sub + 2) * BLOCK]
        kv_sw = jnp.concatenate([kv[:, h:2 * h], kv[:, 0:h], kv[:, 3 * h:4 * h], kv[:, 2 * h:3 * h]],
                                axis=1)
        placed = {
            (0, 0): jnp.where(lo, kv, zero), (0, 1): jnp.where(hi, kv_sw, zero),
            (1, 0): jnp.where(lo, kv_sw, zero), (1, 1): jnp.where(hi, kv, zero),
        }
        sq = jnp.zeros((BLOCK, LANES), jnp.float32)
        for kvh in range(N_KV_HEADS):
            chunks = [CHUNKS_PER_KV * kvh + j for j in range(CHUNKS_PER_KV)]
            q_stack = jnp.concatenate([q[:, c * LANES:(c + 1) * LANES] for c in chunks], axis=0)
            out = None
            for par in range(2):
                kx = placed[(kvh, par)][:, :LANES]
                vx = jnp.concatenate([placed[(kvh, par)][:, LANES:], ones], axis=1)
                s = _dot_nt(q_stack, kx) + bias[0, 2 * kvh + par]
                s0, s1 = s[:, :BLOCK], s[:, BLOCK:]
                m = jnp.max(jnp.maximum(s0, s1), axis=1, keepdims=True)
                m = jnp.broadcast_to(m, s0.shape)
                p = jnp.concatenate([jnp.exp(s0 - m), jnp.exp(s1 - m)], axis=1).astype(jnp.bfloat16)
                r = _dot(p, vx)
                contrib = r[:, :LANES] * (1.0 / r[:, LANES:])
                out = contrib if out is None else out + contrib
            for j, c in enumerate(chunks):
                oc = out[j * BLOCK:(j + 1) * BLOCK]
                acc_ref[rows, c * LANES:(c + 1) * LANES] = oc
                sq = sq + oc * oc
        scale = lax.rsqrt(jnp.sum(sq, axis=1, keepdims=True) * (1.0 / ATTN_WIDTH) + EPS)
        o_ref[rows, :] = (acc_ref[rows, :] * scale * gain_ref[...]).astype(jnp.bfloat16)


def _attention(main, bias, gain, batch, seq, nsub):
    nb = seq // BLOCK
    ns = nb // nsub
    kv_col = (ATTN_WIDTH + 3 * RNN_WIDTH) // (2 * KV_WIDTH)
    bias_shape = (1,) + bias.shape[1:]
    return pl.pallas_call(
        functools.partial(_attn_kernel, nsub=nsub),
        out_shape=jax.ShapeDtypeStruct((batch * seq, ATTN_WIDTH), jnp.bfloat16),
        grid=(batch, ns),
        in_specs=[
            pl.BlockSpec((nsub * BLOCK, ATTN_WIDTH), lambda b, n: (b * ns + n, 0)),
            pl.BlockSpec((BLOCK, 2 * KV_WIDTH),
                         lambda b, n: (b * nb + jnp.maximum(nsub * n - 1, 0), kv_col)),
            pl.BlockSpec((nsub * BLOCK, 2 * KV_WIDTH), lambda b, n: (b * ns + n, kv_col)),
            pl.BlockSpec(bias_shape, lambda b, n: (jnp.minimum(n, 1), 0, 0, 0)),
            pl.BlockSpec(bias_shape, lambda b, n: (1, 0, 0, 0)),
            pl.BlockSpec((1, ATTN_WIDTH), lambda b, n: (0, 0)),
        ],
        out_specs=pl.BlockSpec((nsub * BLOCK, ATTN_WIDTH), lambda b, n: (b * ns + n, 0)),
        scratch_shapes=[pltpu.VMEM((nsub * BLOCK, ATTN_WIDTH), jnp.float32)],
        compiler_params=pltpu.CompilerParams(
            dimension_semantics=("arbitrary", "arbitrary"), vmem_limit_bytes=VMEM_LIMIT),
        name="swa_attention",
    )(main, main, main, bias, bias, gain)


def _attention_bias(sinks):
    q_pos = np.arange(BLOCK) + BLOCK
    k_pos = np.arange(2 * BLOCK)
    dist = (q_pos[:, None] - k_pos[None, :]).astype(np.float32)
    band = (dist >= 0) & (dist < BLOCK)
    assert not band[:, 0].any()
    valid = np.stack([band & (k_pos >= BLOCK)[None, :], band])
    slopes = jnp.exp2(-8.0 * jnp.arange(1, N_Q_HEADS + 1, dtype=jnp.float32) / N_Q_HEADS)
    alibi = -slopes[None, :, None, None] * jnp.asarray(dist)[None, None]
    bias = jnp.where(jnp.asarray(valid)[:, None], alibi, NEG_BIG)
    sink_slot = jnp.asarray(k_pos == 0)[None, None, None, :]
    bias = jnp.where(sink_slot, sinks.astype(jnp.float32)[None, :, None, None], bias)
    bias = bias.reshape(2, N_KV_HEADS, CHUNKS_PER_KV, 2, BLOCK, 2 * BLOCK)
    return bias.transpose(0, 1, 3, 2, 4, 5).reshape(
        2, 2 * N_KV_HEADS, CHUNKS_PER_KV * BLOCK, 2 * BLOCK)


def _hgrn_tables():
    c = CHUNK
    rows = []
    masks = []
    t = np.arange(c)
    for lvl in range(N_LEVELS):
        h = c >> (lvl + 1)
        odd = (t // h) % 2 == 1
        m = np.zeros((c, c), np.float32)
        for r in range(c):
            if odd[r]:
                m[r, (r // h) * h:r + 1] = 1.0
            else:
                m[r, r + 1:(r // h + 1) * h] = 1.0
        rows.append(m)
        same = (t[:, None] // (2 * h)) == (t[None, :] // (2 * h))
        masks.append((same & odd[:, None] & (~odd)[None, :]).astype(np.float32))
    rows.append(np.tril(np.ones((c, c), np.float32)))
    rows.append(np.triu(np.ones((c, c), np.float32), 1))
    masks.append(np.eye(c, dtype=np.float32))
    mall = np.concatenate(rows, axis=0)
    masks = np.stack(masks)
    return np.concatenate([mall, mall], axis=1), np.concatenate([masks, masks], axis=2)


def _hgrn_kernel(lbl_ref, q_ref, i_ref, g_ref, f_ref, mall_ref, mask_ref, gain_ref, *rest,
                 n_chunks, n_cast):
    cast_in = rest[:n_cast]
    o_ref = rest[n_cast]
    cast_out = rest[n_cast + 1:2 * n_cast + 1]
    st_ref = rest[2 * n_cast + 1]

    @pl.when(pl.program_id(1) == 0)
    def _():
        st_ref[...] = jnp.zeros_like(st_ref)

    lbl = lbl_ref[...]
    ex = jnp.exp(lbl - jnp.max(lbl, axis=0, keepdims=True))
    lb = ex[0:1] / jnp.sum(ex, axis=0, keepdims=True)
    gain = gain_ref[...]
    d, c = RNN_HEAD_DIM, CHUNK
    bf16 = jnp.bfloat16
    zrow = jnp.zeros((c, d), bf16)
    zst = jnp.zeros((d, d), bf16)

    def blockdiag(a, b, z):
        return jnp.concatenate([jnp.concatenate([a, z], axis=1),
                                jnp.concatenate([z, b], axis=1)], axis=0)

    def chunk(ci, carry):
        r0 = pl.multiple_of(ci * c, c)
        rows = pl.ds(r0, c)
        fl = f_ref[rows, :]
        f = lb + (1.0 - lb) * (1.0 / (1.0 + jnp.exp(-fl)))
        lf = jnp.log(f)
        kb = (1.0 - f).astype(bf16)
        hi = lf.astype(bf16)
        lo = (lf - hi.astype(jnp.float32)).astype(bf16)
        x = _dot(mall_ref[...], jnp.concatenate([hi, lo], axis=0))
        e = jnp.exp(x).astype(bf16)
        decay = jnp.exp(x[(N_LEVELS + 1) * c - 1:(N_LEVELS + 1) * c, :])
        qx = q_ref[rows, :].astype(jnp.float32)
        qf = (qx * (1.0 / (1.0 + jnp.exp(-qx)))).astype(bf16)
        v = i_ref[rows, :]
        lvl = lambda l: e[l * c:(l + 1) * c]
        ql = [qf * lvl(l) for l in range(N_LEVELS)] + [qf]
        kl = [kb * lvl(l) for l in range(N_LEVELS)] + [kb]
        qb = qf * lvl(N_LEVELS)
        kr = kb * lvl(N_LEVELS + 1)
        outs = []
        for p in range(N_RNN_HEADS // 2):
            c0, c1, c2 = 2 * p * d, (2 * p + 1) * d, (2 * p + 2) * d
            att = None
            for l in range(N_LEVELS + 1):
                rhs = blockdiag(kl[l][:, c0:c1], kl[l][:, c1:c2], zrow)
                r = mask_ref[l] * _dot_nt(ql[l][:, c0:c2], rhs)
                att = r if att is None else att + r
            st0, st1 = st_ref[2 * p], st_ref[2 * p + 1]
            o = (_dot(att.astype(bf16), blockdiag(v[:, c0:c1], v[:, c1:c2], zrow))
                 + _dot_nt(qb[:, c0:c2], blockdiag(st0.astype(bf16), st1.astype(bf16), zst)))
            outs.append(o)
            st_ref[2 * p] = st0 * decay[:, c0:c1] + _dot_tn(v[:, c0:c1], kr[:, c0:c1])
            st_ref[2 * p + 1] = st1 * decay[:, c1:c2] + _dot_tn(v[:, c1:c2], kr[:, c1:c2])
        scales = []
        for hd in range(N_RNN_HEADS):
            oh = outs[hd // 2][:, (hd % 2) * d:(hd % 2 + 1) * d]
            scales.append(jnp.broadcast_to(_rms_scale(oh, d), (c, d)))
        gx = g_ref[rows, :].astype(jnp.float32)
        gate = gx * (1.0 / (1.0 + jnp.exp(-gx)))
        o_all = jnp.concatenate(outs, axis=1) * jnp.concatenate(scales, axis=1)
        o_ref[rows, :] = (o_all * gain * gate).astype(bf16)
        for src, dst in zip(cast_in, cast_out):
            n = src.shape[0] // n_chunks
            wrows = pl.ds(pl.multiple_of(ci * n, n), n)
            dst[wrows, :] = src[wrows, :].astype(bf16)
        return carry

    lax.fori_loop(0, n_chunks, chunk, 0, unroll=4)


def _hgrn(main, f_r, lb_logits, gain, weights, batch, seq, ts):
    mall, masks = _hgrn_tables()
    mall = jnp.asarray(mall, jnp.bfloat16)
    masks = jnp.asarray(masks)
    ns = seq // ts
    n_steps = batch * ns
    n_chunks = ts // CHUNK
    for w in weights:
        assert w.shape[0] % (n_steps * n_chunks * 16) == 0, w.shape
    blk = lambda col: pl.BlockSpec((ts, RNN_WIDTH), lambda b, s: (b * ns + s, col))
    const = lambda shape: pl.BlockSpec(shape, lambda b, s: (0,) * len(shape))
    wblk = lambda w: pl.BlockSpec((w.shape[0] // n_steps, w.shape[1]), lambda b, s: (b * ns + s, 0))
    outs = pl.pallas_call(
        functools.partial(_hgrn_kernel, n_chunks=n_chunks, n_cast=len(weights)),
        out_shape=[jax.ShapeDtypeStruct((batch * seq, RNN_WIDTH), jnp.bfloat16)]
        + [jax.ShapeDtypeStruct(w.shape, jnp.bfloat16) for w in weights],
        grid=(batch, ns),
        in_specs=[
            const(lb_logits.shape),
            blk(1), blk(2), blk(3),
            pl.BlockSpec((ts, RNN_WIDTH), lambda b, s: (b * ns + s, 0)),
            const(mall.shape), const(masks.shape), const(gain.shape),
        ] + [wblk(w) for w in weights],
        out_specs=[pl.BlockSpec((ts, RNN_WIDTH), lambda b, s: (b * ns + s, 0))]
        + [wblk(w) for w in weights],
        scratch_shapes=[pltpu.VMEM((N_RNN_HEADS, RNN_HEAD_DIM, RNN_HEAD_DIM), jnp.float32)],
        compiler_params=pltpu.CompilerParams(
            dimension_semantics=("arbitrary", "arbitrary"), vmem_limit_bytes=VMEM_LIMIT),
        name="hgrn2",
    )(lb_logits, main, main, main, f_r, mall, masks, gain, *weights)
    return outs[0], outs[1:]


def _outproj_kernel(a_ref, r_ref, x_ref, w_ref, post_ref, pre_ref, x1_ref, h_ref):
    mixed = _dot(a_ref[...], w_ref[:ATTN_WIDTH, :]) + _dot(r_ref[...], w_ref[ATTN_WIDTH:, :])
    x1 = x_ref[...] + mixed * _rms_scale(mixed, D_MODEL) * post_ref[...]
    x1_ref[...] = x1
    h_ref[...] = (x1 * _rms_scale(x1, D_MODEL) * pre_ref[...]).astype(jnp.bfloat16)


def _outproj(attn, rnn, x2, w_out, post_gain, pre_gain, tm):
    t = x2.shape[0]
    row = lambda w: pl.BlockSpec((tm, w), lambda i: (i, 0))
    const = lambda shape, **kw: pl.BlockSpec(shape, lambda i: (0, 0), **kw)
    return pl.pallas_call(
        _outproj_kernel,
        out_shape=(jax.ShapeDtypeStruct((t, D_MODEL), jnp.float32),
                   jax.ShapeDtypeStruct((t, D_MODEL), jnp.bfloat16)),
        grid=(t // tm,),
        in_specs=[row(ATTN_WIDTH), row(RNN_WIDTH), row(D_MODEL),
                  const(w_out.shape, pipeline_mode=pl.Buffered(1)),
                  const((1, D_MODEL)), const((1, D_MODEL))],
        out_specs=(row(D_MODEL), row(D_MODEL)),
        compiler_params=pltpu.CompilerParams(
            dimension_semantics=("arbitrary",), vmem_limit_bytes=VMEM_LIMIT),
        name="outproj",
    )(attn, rnn, x2, w_out, post_gain, pre_gain)


def _mlp_kernel(h_ref, wu_ref, wd_ref, x1_ref, gain_ref, o_ref, acc_ref):
    f = pl.program_id(1)

    @pl.when(f == 0)
    def _():
        acc_ref[...] = jnp.zeros_like(acc_ref)

    u = jnp.maximum(_dot(h_ref[...], wu_ref[...]), 0.0)
    acc_ref[...] += _dot((u * u).astype(jnp.bfloat16), wd_ref[...])

    @pl.when(f == pl.num_programs(1) - 1)
    def _():
        y = acc_ref[...]
        o_ref[...] = x1_ref[...] + y * _rms_scale(y, D_MODEL) * gain_ref[...]


def _mlp(h2, w_up, w_down, x1, gain, tm, tf):
    t = h2.shape[0]
    return pl.pallas_call(
        _mlp_kernel,
        out_shape=jax.ShapeDtypeStruct((t, D_MODEL), jnp.float32),
        grid=(t // tm, D_FF // tf),
        in_specs=[
            pl.BlockSpec((tm, D_MODEL), lambda i, f: (i, 0)),
            pl.BlockSpec((D_MODEL, tf), lambda i, f: (0, f)),
            pl.BlockSpec((tf, D_MODEL), lambda i, f: (f, 0)),
            pl.BlockSpec((tm, D_MODEL), lambda i, f: (i, 0)),
            pl.BlockSpec((1, D_MODEL), lambda i, f: (0, 0)),
        ],
        out_specs=pl.BlockSpec((tm, D_MODEL), lambda i, f: (i, 0)),
        scratch_shapes=[pltpu.VMEM((tm, D_MODEL), jnp.float32)],
        compiler_params=pltpu.CompilerParams(
            dimension_semantics=("arbitrary", "arbitrary"), vmem_limit_bytes=VMEM_LIMIT),
        name="mlp",
    )(h2, w_up, w_down, x1, gain)


def kernel(x, w_in, attn_sinks, attn_out_gain, rnn_lb_logits, rnn_norm_gain, w_out,
           mix_pre_gain, mix_post_gain, mlp_pre_gain, mlp_post_gain, w_up, w_down):
    batch, seq, _ = x.shape
    layer = 0
    x2 = x.reshape(batch * seq, D_MODEL)
    main, f_r = _inproj(x2, mix_pre_gain[layer][None], w_in[layer].astype(jnp.bfloat16), tm=512)
    attn = _attention(main, _attention_bias(attn_sinks[layer]), attn_out_gain[layer][None],
                      batch, seq, nsub=4)
    rnn, (w_out_b, w_up_b, w_down_b) = _hgrn(
        main, f_r, rnn_lb_logits, jnp.tile(rnn_norm_gain[layer], N_RNN_HEADS)[None],
        (w_out[layer], w_up[layer], w_down[layer]), batch, seq, ts=512)
    x1, h2 = _outproj(attn, rnn, x2, w_out_b,
                      mix_post_gain[layer][None], mlp_pre_gain[layer][None], tm=512)
    out = _mlp(h2, w_up_b, w_down_b, x1, mlp_post_gain[layer][None], tm=512, tf=1024)
    return out.reshape(batch, seq, D_MODEL)
```

```python
import functools

import numpy as np
import jax
import jax.numpy as jnp
from jax import lax
from jax.experimental import pallas as pl
from jax.experimental.pallas import tpu as pltpu

D_MODEL = 2048
ATTN_WIDTH = 1024
HEAD_DIM = 64
N_Q_HEADS = 16
N_KV_HEADS = 2
KV_WIDTH = 128
BLOCK = 128
RNN_WIDTH = 1024
RNN_HEAD_DIM = 128
N_RNN_HEADS = 8
CHUNK = 64
D_FF = 8192
EPS = 1e-6

LANES = 128
SUBLANES = 8
CHUNKS_PER_KV = ATTN_WIDTH // LANES // N_KV_HEADS
VMEM_LIMIT = 56 * 1024 * 1024

MAIN_WIDTH = ATTN_WIDTH + 3 * RNN_WIDTH + 2 * KV_WIDTH
_KV_SRC = ATTN_WIDTH
_QR_SRC = _KV_SRC + 2 * KV_WIDTH
_F_SRC = _QR_SRC + RNN_WIDTH
_I_SRC = _F_SRC + RNN_WIDTH
_G_SRC = _I_SRC + RNN_WIDTH
_MAIN_PIECES = ((0, ATTN_WIDTH), (_QR_SRC, RNN_WIDTH), (_I_SRC, RNN_WIDTH), (_G_SRC, RNN_WIDTH),
                (_KV_SRC, 2 * KV_WIDTH))
N_LEVELS = 6
NEG_BIG = -1e30

_NT = (((1,), (1,)), ((), ()))
_TN = (((0,), (0,)), ((), ()))


def _dot(a, b):
    return jnp.dot(a, b, preferred_element_type=jnp.float32)


def _dot_nt(a, b):
    return lax.dot_general(a, b, _NT, preferred_element_type=jnp.float32)


def _dot_tn(a, b):
    return lax.dot_general(a, b, _TN, preferred_element_type=jnp.float32)


def _neg_abs(x):
    bits = lax.bitcast_convert_type(x, jnp.uint32) | jnp.uint32(0x80000000)
    return lax.bitcast_convert_type(bits, jnp.float32)


def _rms_scale(y, width):
    return lax.rsqrt(jnp.sum(y * y, axis=-1, keepdims=True) * (1.0 / width) + EPS)


def _inproj_kernel(x_ref, gain_ref, w_ref, main_ref, f_ref):
    x = x_ref[...]
    h = (x * _rms_scale(x, D_MODEL) * gain_ref[...]).astype(jnp.bfloat16)
    dst = 0
    for src, wdt in _MAIN_PIECES:
        main_ref[:, dst:dst + wdt] = _dot(h, w_ref[:, src:src + wdt]).astype(jnp.bfloat16)
        dst += wdt
    f_ref[...] = _dot(h, w_ref[:, _F_SRC:_F_SRC + RNN_WIDTH])


def _inproj(x2, gain, w, tm):
    t = x2.shape[0]
    n_all = w.shape[1]
    return pl.pallas_call(
        _inproj_kernel,
        out_shape=(jax.ShapeDtypeStruct((t, MAIN_WIDTH), jnp.bfloat16),
                   jax.ShapeDtypeStruct((t, RNN_WIDTH), jnp.float32)),
        grid=(t // tm,),
        in_specs=[
            pl.BlockSpec((tm, D_MODEL), lambda i: (i, 0)),
            pl.BlockSpec((1, D_MODEL), lambda i: (0, 0)),
            pl.BlockSpec((D_MODEL, n_all), lambda i: (0, 0), pipeline_mode=pl.Buffered(1)),
        ],
        out_specs=(pl.BlockSpec((tm, MAIN_WIDTH), lambda i: (i, 0)),
                   pl.BlockSpec((tm, RNN_WIDTH), lambda i: (i, 0))),
        compiler_params=pltpu.CompilerParams(
            dimension_semantics=("arbitrary",), vmem_limit_bytes=VMEM_LIMIT),
        name="inproj",
    )(x2, gain, w)


def _attn_kernel(q_ref, kvp_ref, kvc_ref, bias0_ref, bias_ref, gain_ref, o_ref, acc_ref, *, nsub):
    h = HEAD_DIM
    kv_all = jnp.concatenate([kvp_ref[...], kvc_ref[...]], axis=0)
    lane = lax.broadcasted_iota(jnp.int32, (2 * BLOCK, 4 * h), 1)
    row = lax.broadcasted_iota(jnp.int32, (2 * BLOCK, 4 * h), 0)
    lo = ((lane % LANES) < h) & (row > 0)
    hi = ((lane % LANES) >= h) & (row > 0)
    zero = jnp.zeros((2 * BLOCK, 4 * h), jnp.bfloat16)
    ones = jnp.ones((2 * BLOCK, LANES), jnp.bfloat16)
    for sub in range(nsub):
        rows = slice(sub * BLOCK, (sub + 1) * BLOCK)
        bias = bias0_ref if sub == 0 else bias_ref
        q = q_ref[rows, :] * jnp.bfloat16(h ** -0.5)
        kv = kv_all[sub * BLOCK:(sub + 2) * BLOCK]
        kv_sw = jnp.concatenate([kv[:, h:2 * h], kv[:, 0:h], kv[:, 3 * h:4 * h], kv[:, 2 * h:3 * h]],
                                axis=1)
        placed = {
            (0, 0): jnp.where(lo, kv, zero), (0, 1): jnp.where(hi, kv_sw, zero),
            (1, 0): jnp.where(lo, kv_sw, zero), (1, 1): jnp.where(hi, kv, zero),
        }
        sq = jnp.zeros((BLOCK, LANES), jnp.float32)
        for kvh in range(N_KV_HEADS):
            chunks = [CHUNKS_PER_KV * kvh + j for j in range(CHUNKS_PER_KV)]
            q_stack = jnp.concatenate([q[:, c * LANES:(c + 1) * LANES] for c in chunks], axis=0)
            out = None
            for par in range(2):
                kx = placed[(kvh, par)][:, :LANES]
                vx = jnp.concatenate([placed[(kvh, par)][:, LANES:], ones], axis=1)
                s = _dot_nt(q_stack, kx) + bias[0, 2 * kvh + par]
                s0, s1 = s[:, :BLOCK], s[:, BLOCK:]
                m = jnp.max(jnp.maximum(s0, s1), axis=1, keepdims=True)
                m = jnp.broadcast_to(m, s0.shape)
                p = jnp.concatenate([jnp.exp(s0 - m), jnp.exp(s1 - m)], axis=1).astype(jnp.bfloat16)
                r = _dot(p, vx)
                contrib = r[:, :LANES] * (1.0 / r[:, LANES:])
                out = contrib if out is None else out + contrib
            for j, c in enumerate(chunks):
                oc = out[j * BLOCK:(j + 1) * BLOCK]
                acc_ref[rows, c * LANES:(c + 1) * LANES] = oc
                sq = sq + oc * oc
        scale = lax.rsqrt(jnp.sum(sq, axis=1, keepdims=True) * (1.0 / ATTN_WIDTH) + EPS)
        o_ref[rows, :] = (acc_ref[rows, :] * scale * gain_ref[...]).astype(jnp.bfloat16)


def _attention(main, bias, gain, batch, seq, nsub):
    nb = seq // BLOCK
    ns = nb // nsub
    kv_col = (ATTN_WIDTH + 3 * RNN_WIDTH) // (2 * KV_WIDTH)
    bias_shape = (1,) + bias.shape[1:]
    return pl.pallas_call(
        functools.partial(_attn_kernel, nsub=nsub),
        out_shape=jax.ShapeDtypeStruct((batch * seq, ATTN_WIDTH), jnp.bfloat16),
        grid=(batch, ns),
        in_specs=[
            pl.BlockSpec((nsub * BLOCK, ATTN_WIDTH), lambda b, n: (b * ns + n, 0)),
            pl.BlockSpec((BLOCK, 2 * KV_WIDTH),
                         lambda b, n: (b * nb + jnp.maximum(nsub * n - 1, 0), kv_col)),
            pl.BlockSpec((nsub * BLOCK, 2 * KV_WIDTH), lambda b, n: (b * ns + n, kv_col)),
            pl.BlockSpec(bias_shape, lambda b, n: (jnp.minimum(n, 1), 0, 0, 0)),
            pl.BlockSpec(bias_shape, lambda b, n: (1, 0, 0, 0)),
            pl.BlockSpec((1, ATTN_WIDTH), lambda b, n: (0, 0)),
        ],
        out_specs=pl.BlockSpec((nsub * BLOCK, ATTN_WIDTH), lambda b, n: (b * ns + n, 0)),
        scratch_shapes=[pltpu.VMEM((nsub * BLOCK, ATTN_WIDTH), jnp.float32)],
        compiler_params=pltpu.CompilerParams(
            dimension_semantics=("arbitrary", "arbitrary"), vmem_limit_bytes=VMEM_LIMIT),
        name="swa_attention",
    )(main, main, main, bias, bias, gain)


def _attention_bias(sinks):
    q_pos = np.arange(BLOCK) + BLOCK
    k_pos = np.arange(2 * BLOCK)
    dist = (q_pos[:, None] - k_pos[None, :]).astype(np.float32)
    band = (dist >= 0) & (dist < BLOCK)
    assert not band[:, 0].any()
    valid = np.stack([band & (k_pos >= BLOCK)[None, :], band])
    slopes = jnp.exp2(-8.0 * jnp.arange(1, N_Q_HEADS + 1, dtype=jnp.float32) / N_Q_HEADS)
    alibi = -slopes[None, :, None, None] * jnp.asarray(dist)[None, None]
    bias = jnp.where(jnp.asarray(valid)[:, None], alibi, NEG_BIG)
    sink_slot = jnp.asarray(k_pos == 0)[None, None, None, :]
    bias = jnp.where(sink_slot, sinks.astype(jnp.float32)[None, :, None, None], bias)
    bias = bias.reshape(2, N_KV_HEADS, CHUNKS_PER_KV, 2, BLOCK, 2 * BLOCK)
    return bias.transpose(0, 1, 3, 2, 4, 5).reshape(
        2, 2 * N_KV_HEADS, CHUNKS_PER_KV * BLOCK, 2 * BLOCK)


def _hgrn_tables():
    c = CHUNK
    masks = []
    t = np.arange(c)
    for lvl in range(N_LEVELS):
        h = c >> (lvl + 1)
        odd = (t // h) % 2 == 1
        same = (t[:, None] // (2 * h)) == (t[None, :] // (2 * h))
        masks.append((same & odd[:, None] & (~odd)[None, :]).astype(np.float32))
    masks.append(np.eye(c, dtype=np.float32))
    masks = np.stack(masks)
    cum = np.tril(np.ones((c, c), np.float32))
    return np.concatenate([cum, cum], axis=1), np.concatenate([masks, masks], axis=2)


def _hgrn_kernel(lbl_ref, q_ref, i_ref, g_ref, f_ref, cum_ref, mask_ref, gain_ref, *rest,
                 n_chunks, n_cast):
    cast_in = rest[:n_cast]
    o_ref = rest[n_cast]
    cast_out = rest[n_cast + 1:2 * n_cast + 1]
    st_ref = rest[2 * n_cast + 1]

    @pl.when(pl.program_id(1) == 0)
    def _():
        st_ref[...] = jnp.zeros_like(st_ref)

    lbl = lbl_ref[...]
    ex = jnp.exp(lbl - jnp.max(lbl, axis=0, keepdims=True))
    lb = ex[0:1] / jnp.sum(ex, axis=0, keepdims=True)
    gain = gain_ref[...]
    d, c = RNN_HEAD_DIM, CHUNK
    bf16 = jnp.bfloat16
    zrow = jnp.zeros((c, d), bf16)
    zst = jnp.zeros((d, d), bf16)
    sub = lax.broadcasted_iota(jnp.int32, (c, RNN_WIDTH), 0)
    sub3 = lax.broadcasted_iota(jnp.int32, (c // SUBLANES, SUBLANES, RNN_WIDTH), 1)

    def blockdiag(a, b, z):
        return jnp.concatenate([jnp.concatenate([a, z], axis=1),
                                jnp.concatenate([z, b], axis=1)], axis=0)

    def chunk(ci, carry):
        r0 = pl.multiple_of(ci * c, c)
        rows = pl.ds(r0, c)
        fl = f_ref[rows, :]
        f = lb + (1.0 - lb) * (1.0 / (1.0 + jnp.exp(-fl)))
        lf = jnp.log2(f)
        kb = (1.0 - f).astype(bf16)
        hi = lf.astype(bf16)
        lo = (lf - hi.astype(jnp.float32)).astype(bf16)
        b = _dot(cum_ref[...], jnp.concatenate([hi, lo], axis=0))
        decay = jnp.exp2(b[c - 1:c, :])

        def level_factor(l):
            h = c >> (l + 1)
            if h == 1:
                return jnp.where(sub % 2 == 1, f, 1.0).astype(bf16)
            if 2 * h >= SUBLANES:
                bm = jnp.concatenate(
                    [jnp.broadcast_to(b[p + h - 1:p + h, :], (2 * h, b.shape[1]))
                     for p in range(0, c, 2 * h)], axis=0)
            else:
                b3 = b.reshape(c // SUBLANES, SUBLANES, b.shape[1])
                bm = jnp.where(sub3 < 2 * h,
                               jnp.broadcast_to(b3[:, h - 1:h, :], b3.shape),
                               jnp.broadcast_to(b3[:, 3 * h - 1:3 * h, :], b3.shape)).reshape(b.shape)
            return jnp.exp2(_neg_abs(b - bm)).astype(bf16)

        qx = q_ref[rows, :].astype(jnp.float32)
        qf = (qx * (1.0 / (1.0 + jnp.exp(-qx)))).astype(bf16)
        v = i_ref[rows, :]
        lvls = [level_factor(l) for l in range(N_LEVELS)]
        ql = [qf * e for e in lvls] + [qf]
        kl = [kb * e for e in lvls] + [kb]
        qb = qf * jnp.exp2(b).astype(bf16)
        kr = kb * jnp.exp2(b[c - 1:c, :] - b).astype(bf16)
        outs = []
        for p in range(N_RNN_HEADS // 2):
            c0, c1, c2 = 2 * p * d, (2 * p + 1) * d, (2 * p + 2) * d
            att = None
            for l in range(N_LEVELS + 1):
                rhs = blockdiag(kl[l][:, c0:c1], kl[l][:, c1:c2], zrow)
                r = mask_ref[l] * _dot_nt(ql[l][:, c0:c2], rhs)
                att = r if att is None else att + r
            st0, st1 = st_ref[2 * p], st_ref[2 * p + 1]
            o = (_dot(att.astype(bf16), blockdiag(v[:, c0:c1], v[:, c1:c2], zrow))
                 + _dot_nt(qb[:, c0:c2], blockdiag(st0.astype(bf16), st1.astype(bf16), zst)))
            outs.append(o)
            st_ref[2 * p] = st0 * decay[:, c0:c1] + _dot_tn(v[:, c0:c1], kr[:, c0:c1])
            st_ref[2 * p + 1] = st1 * decay[:, c1:c2] + _dot_tn(v[:, c1:c2], kr[:, c1:c2])
        scales = []
        for hd in range(N_RNN_HEADS):
            oh = outs[hd // 2][:, (hd % 2) * d:(hd % 2 + 1) * d]
            scales.append(jnp.broadcast_to(_rms_scale(oh, d), (c, d)))
        gx = g_ref[rows, :].astype(jnp.float32)
        gate = gx * (1.0 / (1.0 + jnp.exp(-gx)))
        o_all = jnp.concatenate(outs, axis=1) * jnp.concatenate(scales, axis=1)
        o_ref[rows, :] = (o_all * gain * gate).astype(bf16)
        for src, dst in zip(cast_in, cast_out):
            n = src.shape[0] // n_chunks
            wrows = pl.ds(pl.multiple_of(ci * n, n), n)
            dst[wrows, :] = src[wrows, :].astype(bf16)
        return carry

    lax.fori_loop(0, n_chunks, chunk, 0, unroll=4)


def _hgrn(main, f_r, lb_logits, gain, weights, batch, seq, ts):
    cum, masks = _hgrn_tables()
    cum = jnp.asarray(cum, jnp.bfloat16)
    masks = jnp.asarray(masks)
    ns = seq // ts
    n_steps = batch * ns
    n_chunks = ts // CHUNK
    for w in weights:
        assert w.shape[0] % (n_steps * n_chunks * 16) == 0, w.shape
    blk = lambda col: pl.BlockSpec((ts, RNN_WIDTH), lambda b, s: (b * ns + s, col))
    const = lambda shape: pl.BlockSpec(shape, lambda b, s: (0,) * len(shape))
    wblk = lambda w: pl.BlockSpec((w.shape[0] // n_steps, w.shape[1]), lambda b, s: (b * ns + s, 0))
    outs = pl.pallas_call(
        functools.partial(_hgrn_kernel, n_chunks=n_chunks, n_cast=len(weights)),
        out_shape=[jax.ShapeDtypeStruct((batch * seq, RNN_WIDTH), jnp.bfloat16)]
        + [jax.ShapeDtypeStruct(w.shape, jnp.bfloat16) for w in weights],
        grid=(batch, ns),
        in_specs=[
            const(lb_logits.shape),
            blk(1), blk(2), blk(3),
            pl.BlockSpec((ts, RNN_WIDTH), lambda b, s: (b * ns + s, 0)),
            const(cum.shape), const(masks.shape), const(gain.shape),
        ] + [wblk(w) for w in weights],
        out_specs=[pl.BlockSpec((ts, RNN_WIDTH), lambda b, s: (b * ns + s, 0))]
        + [wblk(w) for w in weights],
        scratch_shapes=[pltpu.VMEM((N_RNN_HEADS, RNN_HEAD_DIM, RNN_HEAD_DIM), jnp.float32)],
        compiler_params=pltpu.CompilerParams(
            dimension_semantics=("arbitrary", "arbitrary"), vmem_limit_bytes=VMEM_LIMIT),
        name="hgrn2",
    )(lb_logits, main, main, main, f_r, cum, masks, gain, *weights)
    return outs[0], outs[1:]


def _outproj_kernel(a_ref, r_ref, x_ref, w_ref, post_ref, pre_ref, x1_ref, h_ref):
    mixed = _dot(a_ref[...], w_ref[:ATTN_WIDTH, :]) + _dot(r_ref[...], w_ref[ATTN_WIDTH:, :])
    x1 = x_ref[...] + mixed * _rms_scale(mixed, D_MODEL) * post_ref[...]
    x1_ref[...] = x1
    h_ref[...] = (x1 * _rms_scale(x1, D_MODEL) * pre_ref[...]).astype(jnp.bfloat16)


def _outproj(attn, rnn, x2, w_out, post_gain, pre_gain, tm):
    t = x2.shape[0]
    row = lambda w: pl.BlockSpec((tm, w), lambda i: (i, 0))
    const = lambda shape, **kw: pl.BlockSpec(shape, lambda i: (0, 0), **kw)
    return pl.pallas_call(
        _outproj_kernel,
        out_shape=(jax.ShapeDtypeStruct((t, D_MODEL), jnp.float32),
                   jax.ShapeDtypeStruct((t, D_MODEL), jnp.bfloat16)),
        grid=(t // tm,),
        in_specs=[row(ATTN_WIDTH), row(RNN_WIDTH), row(D_MODEL),
                  const(w_out.shape, pipeline_mode=pl.Buffered(1)),
                  const((1, D_MODEL)), const((1, D_MODEL))],
        out_specs=(row(D_MODEL), row(D_MODEL)),
        compiler_params=pltpu.CompilerParams(
            dimension_semantics=("arbitrary",), vmem_limit_bytes=VMEM_LIMIT),
        name="outproj",
    )(attn, rnn, x2, w_out, post_gain, pre_gain)


def _mlp_kernel(h_ref, wu_ref, wd_ref, x1_ref, gain_ref, o_ref, acc_ref):
    f = pl.program_id(1)

    @pl.when(f == 0)
    def _():
        acc_ref[...] = jnp.zeros_like(acc_ref)

    u = jnp.maximum(_dot(h_ref[...], wu_ref[...]), 0.0)
    acc_ref[...] += _dot((u * u).astype(jnp.bfloat16), wd_ref[...])

    @pl.when(f == pl.num_programs(1) - 1)
    def _():
        y = acc_ref[...]
        o_ref[...] = x1_ref[...] + y * _rms_scale(y, D_MODEL) * gain_ref[...]


def _mlp(h2, w_up, w_down, x1, gain, tm, tf):
    t = h2.shape[0]
    return pl.pallas_call(
        _mlp_kernel,
        out_shape=jax.ShapeDtypeStruct((t, D_MODEL), jnp.float32),
        grid=(t // tm, D_FF // tf),
        in_specs=[
            pl.BlockSpec((tm, D_MODEL), lambda i, f: (i, 0)),
            pl.BlockSpec((D_MODEL, tf), lambda i, f: (0, f)),
            pl.BlockSpec((tf, D_MODEL), lambda i, f: (f, 0)),
            pl.BlockSpec((tm, D_MODEL), lambda i, f: (i, 0)),
            pl.BlockSpec((1, D_MODEL), lambda i, f: (0, 0)),
        ],
        out_specs=pl.BlockSpec((tm, D_MODEL), lambda i, f: (i, 0)),
        scratch_shapes=[pltpu.VMEM((tm, D_MODEL), jnp.float32)],
        compiler_params=pltpu.CompilerParams(
            dimension_semantics=("arbitrary", "arbitrary"), vmem_limit_bytes=VMEM_LIMIT),
        name="mlp",
    )(h2, w_up, w_down, x1, gain)


def kernel(x, w_in, attn_sinks, attn_out_gain, rnn_lb_logits, rnn_norm_gain, w_out,
           mix_pre_gain, mix_post_gain, mlp_pre_gain, mlp_post_gain, w_up, w_down):
    batch, seq, _ = x.shape
    layer = 0
    x2 = x.reshape(batch * seq, D_MODEL)
    main, f_r = _inproj(x2, mix_pre_gain[layer][None], w_in[layer].astype(jnp.bfloat16), tm=512)
    attn = _attention(main, _attention_bias(attn_sinks[layer]), attn_out_gain[layer][None],
                      batch, seq, nsub=8)
    rnn, (w_out_b, w_up_b, w_down_b) = _hgrn(
        main, f_r, rnn_lb_logits, jnp.tile(rnn_norm_gain[layer], N_RNN_HEADS)[None],
        (w_out[layer], w_up[layer], w_down[layer]), batch, seq, ts=512)
    x1, h2 = _outproj(attn, rnn, x2, w_out_b,
                      mix_post_gain[layer][None], mlp_pre_gain[layer][None], tm=512)
    out = _mlp(h2, w_up_b, w_down_b, x1, mlp_post_gain[layer][None], tm=512, tf=1024)
    return out.reshape(batch, seq, D_MODEL)
```
